```python
import math
import jax, jax.numpy as jnp
from jax import lax
import numpy as np

D_MODEL = 4096
BATCH = 4
SEQ = 2048
DEPTH = 4
DEC_BATCH = 128
DEC_SEQ = 1
PAST_LEN = 16384
PAGE_SIZE = 128

N_BRANCH = 4
BRANCH_WIDTH = D_MODEL // 4
HG_HEADS = 8
HG_DK = BRANCH_WIDTH // HG_HEADS
HG_WIDTH = HG_HEADS * HG_DK
LRU_WIDTH = BRANCH_WIDTH
LRU_BLOCKS = 8
LRU_BW = LRU_WIDTH // LRU_BLOCKS
CONV_W = 4
LRU_C = 8.0
RET_HEADS = 4
RET_DK = BRANCH_WIDTH // RET_HEADS
RET_DV = 2 * RET_DK
RET_QK = RET_HEADS * RET_DK
RET_V = RET_HEADS * RET_DV
ROPE_BASE = 10000.0
GLA_HEADS = 4
GLA_DK = BRANCH_WIDTH // 2 // GLA_HEADS
GLA_DV = BRANCH_WIDTH // GLA_HEADS
GLA_QK = GLA_HEADS * GLA_DK
GLA_V = GLA_HEADS * GLA_DV
GLA_RANK = 16
GLA_TAU = 16.0
D_FF = 256 * ((8 * D_MODEL // 3 + 255) // 256)
GATED_CHUNK = 16
RET_CHUNK = 128
EPS = 1e-6
IN_SPLITS = (HG_WIDTH,) * 4 + (LRU_WIDTH,) * 2 + (RET_QK, RET_QK, RET_V, RET_V) + (GLA_QK, GLA_QK, GLA_V, GLA_V, GLA_RANK)
D_IN = sum(IN_SPLITS)
IN_SPLIT_IDX = tuple(int(v) for v in np.cumsum(IN_SPLITS)[:-1])
BRANCH_OUT = (HG_WIDTH, LRU_WIDTH, RET_V, GLA_V)
D_BR = sum(BRANCH_OUT)
BR_OFF = tuple(int(v) for v in np.cumsum((0,) + BRANCH_OUT))

kernel_name = 'hybrid_hgrn2_rglru_retnet_gla_step'


def _rmsnorm(x, g):
    xf = x.astype(jnp.float32)
    y = xf * lax.rsqrt(jnp.mean(xf * xf, axis=-1, keepdims=True) + EPS)
    return (y * g.astype(jnp.float32)).astype(x.dtype)


def _head_rmsnorm(o, g):
    B, T, H, d = o.shape
    y = o * lax.rsqrt(jnp.mean(o * o, axis=-1, keepdims=True) + EPS)
    return y.reshape(B, T, H * d) * g.astype(jnp.float32)


def _head_groupnorm(o, g):
    B, T, H, d = o.shape
    c = o - jnp.mean(o, axis=-1, keepdims=True)
    y = c * lax.rsqrt(jnp.mean(c * c, axis=-1, keepdims=True) + EPS)
    return y.reshape(B, T, H * d) * g.astype(jnp.float32)


def _swiglu(x, w_up, w_down):
    gate, up = jnp.split(x @ w_up, 2, axis=-1)
    return (jax.nn.silu(gate) * up) @ w_down


def _rotary(x, pos):
    half = x.shape[-1] // 2
    freqs = jnp.exp(-math.log(ROPE_BASE) * jnp.arange(half, dtype=jnp.float32) / half)
    ang = pos.astype(jnp.float32)[:, None] * freqs[None]
    cos = jnp.cos(ang)[None, :, None, :]
    sin = jnp.sin(ang)[None, :, None, :]
    x1, x2 = x[..., :half], x[..., half:]
    return jnp.concatenate([x1 * cos - x2 * sin, x1 * sin + x2 * cos], axis=-1)


def _gated_chunk_scan(q, k, v, log_f, s0, chunk):
    dt = s0.dtype
    B, T, H, K = q.shape
    n = T // chunk
    q, k, v, g = (a.reshape(B, n, chunk, H, a.shape[-1]) for a in (q, k, v, log_f))
    b = jnp.cumsum(g, axis=2)
    b_last = b[:, :, -1]
    causal = jnp.tril(jnp.ones((chunk, chunk), dtype=bool))[:, :, None, None]
    diff = b[:, :, :, None] - b[:, :, None, :]
    decay = jnp.where(causal, jnp.exp(jnp.where(causal, diff, 0.0)), 0.0)
    scores = jnp.einsum('bnihk,bnjhk,bnijhk->bnhij', q, k, decay)
    o_intra = jnp.einsum('bnhij,bnjhv->bnihv', scores, v)
    q_in = q * jnp.exp(b)
    k_out = k * jnp.exp(b_last[:, :, None] - b)

    def step(s, xs):
        qc, kc, vc, dl = xs
        o = jnp.einsum('bchk,bhkv->bchv', qc, s)
        s = s * dl[..., None] + jnp.einsum('bchk,bchv->bhkv', kc, vc)
        return s, o

    xs = tuple(jnp.moveaxis(a, 1, 0) for a in (q_in, k_out, v, jnp.exp(b_last)))
    s_fin, o_inter = lax.scan(step, s0.astype(jnp.float32), xs)
    o = o_intra + jnp.moveaxis(o_inter, 0, 1)
    return o.reshape(B, T, H, v.shape[-1]), s_fin.astype(dt)


def _retention_chunk_scan(q, k, v, log_gamma, s0, chunk):
    dt = s0.dtype
    B, T, H, K = q.shape
    n = T // chunk
    q, k, v = (a.reshape(B, n, chunk, H, a.shape[-1]) for a in (q, k, v))
    idx = jnp.arange(chunk, dtype=jnp.float32)
    dist = idx[:, None] - idx[None, :]
    causal = (dist >= 0)[None]
    expo = jnp.where(causal, dist[None] * log_gamma[:, None, None], 0.0)
    decay = jnp.where(causal, jnp.exp(expo), 0.0)
    scores = jnp.einsum('bnihk,bnjhk->bnhij', q, k) * decay[None, None]
    o_intra = jnp.einsum('bnhij,bnjhv->bnihv', scores, v)
    q_in = q * jnp.exp((idx + 1.0)[:, None] * log_gamma[None])[..., None]
    k_out = k * jnp.exp((chunk - 1.0 - idx)[:, None] * log_gamma[None])[..., None]
    dl = jnp.exp(chunk * log_gamma)[None, :, None, None]

    def step(s, xs):
        qc, kc, vc = xs
        o = jnp.einsum('bchk,bhkv->bchv', qc, s)
        return s * dl + jnp.einsum('bchk,bchv->bhkv', kc, vc), o

    xs = tuple(jnp.moveaxis(a, 1, 0) for a in (q_in, k_out, v))
    s_fin, o_inter = lax.scan(step, s0.astype(jnp.float32), xs)
    o = o_intra + jnp.moveaxis(o_inter, 0, 1)
    return o.reshape(B, T, H, v.shape[-1]), s_fin.astype(dt)


def _lin_combine(left, right):
    a_l, b_l = left
    a_r, b_r = right
    return a_l * a_r, a_r * b_l + b_r


def _token_mixing(h, st, l, pos, P):
    f32 = jnp.float32
    B, T, _ = h.shape
    s_hg, s_lh, s_lc, s_rt, s_gl = st
    z = h @ P['w_in'][l]
    (hq, hf, hi, hg, lx, ly, rq, rk, rv, rg, gq, gk, gv, gg, glr) = jnp.split(z, IN_SPLIT_IDX, axis=-1)

    def heads(a, H):
        return a.astype(f32).reshape(B, T, H, -1)

    p_lb = jax.nn.softmax(P['hgrn_lb_logits'].astype(f32), axis=0)
    lb = (jnp.cumsum(p_lb, axis=0) - p_lb[0])[l]
    hf32 = hf.astype(f32)
    f_gate = lb + (1.0 - lb) * jax.nn.sigmoid(hf32)
    log_f = jnp.log(f_gate)
    one_minus_f = (1.0 - lb) * jax.nn.sigmoid(-hf32)
    o_a, s_hg_new = _gated_chunk_scan(heads(jax.nn.silu(hq), HG_HEADS), heads(one_minus_f, HG_HEADS),
                                      heads(hi, HG_HEADS), heads(log_f, HG_HEADS), s_hg, math.gcd(T, GATED_CHUNK))
    br_a = _head_rmsnorm(o_a, P['hgrn_norm_g'][l]) * jax.nn.silu(hg.astype(f32))

    xp = jnp.concatenate([s_lc.astype(f32), lx.astype(f32)], axis=1)
    cw = P['rglru_conv_w'][l].astype(f32)
    u = P['rglru_conv_b'][l].astype(f32) + xp[:, 0:T] * cw[0]
    for j in range(1, CONV_W):
        u = u + xp[:, j:j + T] * cw[j]
    s_lc_new = xp[:, T:]
    ub = u.reshape(B, T, LRU_BLOCKS, LRU_BW)
    r = jax.nn.sigmoid(jnp.einsum('btnc,ncd->btnd', ub, P['rglru_w_a'][l].astype(f32)).reshape(B, T, LRU_WIDTH)
                       + P['rglru_b_a'][l].astype(f32))
    i_g = jax.nn.sigmoid(jnp.einsum('btnc,ncd->btnd', ub, P['rglru_w_x'][l].astype(f32)).reshape(B, T, LRU_WIDTH)
                         + P['rglru_b_x'][l].astype(f32))
    log_a = -LRU_C * r * jax.nn.softplus(-P['rglru_lambda'][l].astype(f32))
    a = jnp.exp(log_a)
    mult = jnp.where((pos == 0)[None, :, None], 1.0, jnp.sqrt(jnp.maximum(-jnp.expm1(2.0 * log_a), 0.0)))
    bt = (mult * i_g * u).at[:, 0].add(a[:, 0] * s_lh.astype(f32))
    _, hseq = lax.associative_scan(_lin_combine, (a, bt), axis=1)
    br_b = hseq * jax.nn.gelu(ly.astype(f32))
    s_lh_new = hseq[:, -1]

    log_gamma = jnp.log1p(-jnp.exp2(-5.0 - jnp.arange(RET_HEADS, dtype=f32)))
    q_c = _rotary(heads(rq, RET_HEADS), pos)
    k_c = _rotary(heads(rk, RET_HEADS), pos) * RET_DK ** -0.5
    o_c, s_rt_new = _retention_chunk_scan(q_c, k_c, heads(rv, RET_HEADS), log_gamma, s_rt, math.gcd(T, RET_CHUNK))
    br_c = _head_groupnorm(o_c, P['ret_norm_g'][l]) * jax.nn.silu(rg.astype(f32))

    log_alpha = jax.nn.log_sigmoid(glr.astype(f32) @ P['gla_w_up'][l].astype(f32) + P['gla_b_up'][l].astype(f32)) / GLA_TAU
    o_d, s_gl_new = _gated_chunk_scan(heads(gq, GLA_HEADS) * GLA_DK ** -0.5, heads(gk, GLA_HEADS), heads(gv, GLA_HEADS),
                                      heads(log_alpha, GLA_HEADS), s_gl, math.gcd(T, GATED_CHUNK))
    br_d = _head_rmsnorm(o_d, P['gla_norm_g'][l]) * jax.nn.silu(gg.astype(f32))

    gates = jax.nn.sigmoid((h @ P['w_gate'][l] + P['b_gate'][l]).astype(f32)).reshape(B, T, N_BRANCH, D_MODEL)
    wb = P['w_branch'][l]
    terms = [gates[:, :, n] * (br.astype(h.dtype) @ wb[BR_OFF[n]:BR_OFF[n + 1]])
             for n, br in enumerate((br_a, br_b, br_c, br_d))]
    merged = terms[0] + terms[1] + terms[2] + terms[3]
    out = merged.astype(h.dtype) @ P['w_out'][l]
    new_st = (s_hg_new, s_lh_new.astype(s_lh.dtype), s_lc_new.astype(s_lc.dtype), s_rt_new, s_gl_new)
    return out, new_st


def _layer(x, st, l, pos, P):
    g = P['norm_g'][l]
    x = x + 0.5 * _rmsnorm(_swiglu(_rmsnorm(x, g[0]), P['w_ff_up'][l, 0], P['w_ff_down'][l, 0]), g[1])
    m, st = _token_mixing(_rmsnorm(x, g[2]), st, l, pos, P)
    x = x + _rmsnorm(m, g[3])
    x = x + 0.5 * _rmsnorm(_swiglu(_rmsnorm(x, g[4]), P['w_ff_up'][l, 1], P['w_ff_down'][l, 1]), g[5])
    return x, st


def setup_inputs(seed: int = 0) -> dict:
    key = jax.random.key(seed)
    ks = iter(jax.random.split(key, 40))

    def nrm(shape, scale):
        return jax.random.normal(next(ks), shape, jnp.float32) * scale

    x_prompt = nrm((BATCH, SEQ, D_MODEL), 1.0)
    x_sample = nrm((DEC_BATCH, DEC_SEQ, D_MODEL), 1.0)
    state_hgrn = nrm((DEPTH, DEC_BATCH, HG_HEADS, HG_DK, HG_DK), 0.5)
    state_rglru_h = nrm((DEPTH, DEC_BATCH, LRU_WIDTH), 0.5)
    state_rglru_conv = nrm((DEPTH, DEC_BATCH, CONV_W - 1, LRU_WIDTH), 1.0)
    state_retention = nrm((DEPTH, DEC_BATCH, RET_HEADS, RET_DK, RET_DV), 1.0)
    state_gla = nrm((DEPTH, DEC_BATCH, GLA_HEADS, GLA_DK, GLA_DV), 0.5)
    norm_g = 1.0 + nrm((DEPTH, 6, D_MODEL), 0.02)
    w_ff_up = nrm((DEPTH, 2, D_MODEL, 2 * D_FF), D_MODEL ** -0.5)
    w_ff_down = nrm((DEPTH, 2, D_FF, D_MODEL), D_FF ** -0.5)
    w_in = nrm((DEPTH, D_MODEL, D_IN), D_MODEL ** -0.5)
    hgrn_lb_logits = nrm((DEPTH, HG_WIDTH), 0.5)
    hgrn_norm_g = 1.0 + nrm((DEPTH, HG_WIDTH), 0.02)
    rglru_conv_w = nrm((DEPTH, CONV_W, LRU_WIDTH), CONV_W ** -0.5)
    rglru_conv_b = nrm((DEPTH, LRU_WIDTH), 0.02)
    rglru_w_a = nrm((DEPTH, LRU_BLOCKS, LRU_BW, LRU_BW), LRU_BW ** -0.5)
    rglru_b_a = nrm((DEPTH, LRU_WIDTH), 0.02)
    rglru_w_x = nrm((DEPTH, LRU_BLOCKS, LRU_BW, LRU_BW), LRU_BW ** -0.5)
    rglru_b_x = nrm((DEPTH, LRU_WIDTH), 0.02)
    a_c = jax.random.uniform(next(ks), (DEPTH, LRU_WIDTH), jnp.float32, 0.9, 0.999) ** (1.0 / LRU_C)
    rglru_lambda = jnp.log(a_c) - jnp.log1p(-a_c)
    ret_norm_g = 1.0 + nrm((DEPTH, RET_V), 0.02)
    gla_w_up = nrm((DEPTH, GLA_RANK, GLA_QK), GLA_RANK ** -0.5)
    gla_b_up = nrm((DEPTH, GLA_QK), 0.02)
    gla_norm_g = 1.0 + nrm((DEPTH, GLA_V), 0.02)
    w_branch = jnp.concatenate([nrm((DEPTH, w, D_MODEL), w ** -0.5) for w in BRANCH_OUT], axis=1)
    w_gate = nrm((DEPTH, D_MODEL, N_BRANCH * D_MODEL), D_MODEL ** -0.5)
    b_gate = nrm((DEPTH, N_BRANCH * D_MODEL), 0.02)
    w_out = nrm((DEPTH, D_MODEL, D_MODEL), D_MODEL ** -0.5)
    return {'x_prompt': x_prompt, 'x_sample': x_sample,
            'state_hgrn': state_hgrn, 'state_rglru_h': state_rglru_h, 'state_rglru_conv': state_rglru_conv,
            'state_retention': state_retention, 'state_gla': state_gla,
            'norm_g': norm_g, 'w_ff_up': w_ff_up, 'w_ff_down': w_ff_down, 'w_in': w_in,
            'hgrn_lb_logits': hgrn_lb_logits, 'hgrn_norm_g': hgrn_norm_g,
            'rglru_conv_w': rglru_conv_w, 'rglru_conv_b': rglru_conv_b,
            'rglru_w_a': rglru_w_a, 'rglru_b_a': rglru_b_a, 'rglru_w_x': rglru_w_x, 'rglru_b_x': rglru_b_x,
            'rglru_lambda': rglru_lambda, 'ret_norm_g': ret_norm_g,
            'gla_w_up': gla_w_up, 'gla_b_up': gla_b_up, 'gla_norm_g': gla_norm_g,
            'w_branch': w_branch, 'w_gate': w_gate, 'b_gate': b_gate, 'w_out': w_out}


def reference(x_prompt, x_sample, state_hgrn, state_rglru_h, state_rglru_conv, state_retention, state_gla,
              norm_g, w_ff_up, w_ff_down, w_in, hgrn_lb_logits, hgrn_norm_g,
              rglru_conv_w, rglru_conv_b, rglru_w_a, rglru_b_a, rglru_w_x, rglru_b_x, rglru_lambda,
              ret_norm_g, gla_w_up, gla_b_up, gla_norm_g, w_branch, w_gate, b_gate, w_out):
    P = dict(norm_g=norm_g, w_ff_up=w_ff_up, w_ff_down=w_ff_down, w_in=w_in,
             hgrn_lb_logits=hgrn_lb_logits, hgrn_norm_g=hgrn_norm_g,
             rglru_conv_w=rglru_conv_w, rglru_conv_b=rglru_conv_b, rglru_w_a=rglru_w_a, rglru_b_a=rglru_b_a,
             rglru_w_x=rglru_w_x, rglru_b_x=rglru_b_x, rglru_lambda=rglru_lambda, ret_norm_g=ret_norm_g,
             gla_w_up=gla_w_up, gla_b_up=gla_b_up, gla_norm_g=gla_norm_g,
             w_branch=w_branch, w_gate=w_gate, b_gate=b_gate, w_out=w_out)
    dt = x_prompt.dtype
    bp = x_prompt.shape[0]
    zero_state = (jnp.zeros((bp, HG_HEADS, HG_DK, HG_DK), dt), jnp.zeros((bp, LRU_WIDTH), dt),
                  jnp.zeros((bp, CONV_W - 1, LRU_WIDTH), dt), jnp.zeros((bp, RET_HEADS, RET_DK, RET_DV), dt),
                  jnp.zeros((bp, GLA_HEADS, GLA_DK, GLA_DV), dt))
    pos_p = jnp.arange(x_prompt.shape[1], dtype=jnp.int32)
    pos_s = PAST_LEN + jnp.arange(x_sample.shape[1], dtype=jnp.int32)
    yp, ys = x_prompt, x_sample
    new_p, new_s = [], []
    for l in range(DEPTH):
        yp, stp = _layer(yp, zero_state, l, pos_p, P)
        st_in = (state_hgrn[l], state_rglru_h[l], state_rglru_conv[l], state_retention[l], state_gla[l])
        ys, sts = _layer(ys, st_in, l, pos_s, P)
        new_p.append(stp)
        new_s.append(sts)
    sp = [jnp.stack([s[i] for s in new_p]) for i in range(5)]
    ss = [jnp.stack([s[i] for s in new_s]) for i in range(5)]
    return (yp, ys, sp[0], sp[1], sp[2], sp[3], sp[4], ss[0], ss[1], ss[2], ss[3], ss[4])
```

```python
import functools
import math

import jax
import jax.numpy as jnp
from jax import lax
from jax.experimental import pallas as pl
from jax.experimental.pallas import tpu as pltpu

F32 = jnp.float32
BF16 = jnp.bfloat16

D_MODEL = 4096
DEPTH = 4
PAST_LEN = 16384
N_BRANCH = 4
BRANCH_WIDTH = D_MODEL // 4
HG_HEADS = 8
HG_DK = BRANCH_WIDTH // HG_HEADS
LRU_WIDTH = BRANCH_WIDTH
LRU_BLOCKS = 8
LRU_BW = LRU_WIDTH // LRU_BLOCKS
CONV_W = 4
LRU_C = 8.0
RET_HEADS = 4
RET_DK = BRANCH_WIDTH // RET_HEADS
RET_DV = 2 * RET_DK
ROPE_BASE = 10000.0
GLA_HEADS = 4
GLA_DK = BRANCH_WIDTH // 2 // GLA_HEADS
GLA_DV = BRANCH_WIDTH // GLA_HEADS
GLA_QK = GLA_HEADS * GLA_DK
GLA_RANK = 16
GLA_TAU = 16.0
D_FF = 256 * ((8 * D_MODEL // 3 + 255) // 256)
EPS = 1e-6

OFF_HQ, OFF_HF, OFF_HI, OFF_HG = 0, 1024, 2048, 3072
OFF_LX, OFF_LY = 4096, 5120
OFF_RQ, OFF_RK, OFF_RV, OFF_RG = 6144, 7168, 8192, 10240
OFF_GQ, OFF_GK, OFF_GV, OFF_GG = 12288, 12800, 13312, 14336
OFF_GLR = 15360
D_Z = OFF_GLR

LANES = 128
SUBLANES = 8
VMEM_LIMIT_CAP = 58 * 1024 * 1024
VMEM_SLACK = 12 * 1024 * 1024


def _params(semantics, vmem_bytes):
    return pltpu.CompilerParams(dimension_semantics=semantics,
                                vmem_limit_bytes=int(min(max(vmem_bytes, 16 << 20), VMEM_LIMIT_CAP)))


def _pick(n, candidates):
    for c in candidates:
        if n % c == 0:
            return c
    raise ValueError(f"no tile of {candidates} divides {n}")


def _sigmoid(x):
    return 1.0 / (1.0 + jnp.exp(-x))


def _silu(x):
    return x * _sigmoid(x)


def _dot(a, b):
    return jnp.dot(a, b, preferred_element_type=F32)


def _dot_nt(a, b):
    return lax.dot_general(a, b, (((1,), (1,)), ((), ())), preferred_element_type=F32)


def _rms(x, g):
    return x * lax.rsqrt(jnp.mean(x * x, axis=-1, keepdims=True) + EPS) * g


def _norm_kernel(x_ref, g_ref, h_ref):
    h_ref[...] = _rms(x_ref[...], g_ref[...]).astype(h_ref.dtype)


def _rmsnorm_rows(x, g_row):
    m, d = x.shape
    tm = _pick(m, (320, 160, 128, 64, 32, 16))
    return pl.pallas_call(
        _norm_kernel,
        grid=(m // tm,),
        in_specs=[pl.BlockSpec((tm, d), lambda i: (i, 0)), pl.BlockSpec((1, d), lambda i: (0, 0))],
        out_specs=pl.BlockSpec((tm, d), lambda i: (i, 0)),
        out_shape=jax.ShapeDtypeStruct((m, d), BF16),
        compiler_params=_params(("parallel",), tm * d * 24 + VMEM_SLACK),
        name="rmsnorm",
    )(x, g_row)


def _resid_norm_kernel(x_ref, f_ref, gp_ref, gn_ref, xo_ref, h_ref, *, scale):
    xn = x_ref[...] + scale * _rms(f_ref[...], gp_ref[...])
    xo_ref[...] = xn
    h_ref[...] = _rms(xn, gn_ref[...]).astype(h_ref.dtype)


def _resid_kernel(x_ref, f_ref, gp_ref, xo_ref, *, scale):
    xo_ref[...] = x_ref[...] + scale * _rms(f_ref[...], gp_ref[...])


def _resid_norm(x, f, g_post, scale, g_next):
    m, d = x.shape
    tm = _pick(m, (160, 128, 64, 32, 16))
    row = pl.BlockSpec((tm, d), lambda i: (i, 0))
    vec = pl.BlockSpec((1, d), lambda i: (0, 0))
    if g_next is None:
        return pl.pallas_call(
            functools.partial(_resid_kernel, scale=scale),
            grid=(m // tm,), in_specs=[row, row, vec], out_specs=row,
            out_shape=jax.ShapeDtypeStruct((m, d), F32),
            compiler_params=_params(("parallel",), tm * d * 40 + VMEM_SLACK),
            name="resid",
        )(x, f, g_post), None
    return pl.pallas_call(
        functools.partial(_resid_norm_kernel, scale=scale),
        grid=(m // tm,), in_specs=[row, row, vec, vec], out_specs=[row, row],
        out_shape=[jax.ShapeDtypeStruct((m, d), F32), jax.ShapeDtypeStruct((m, d), BF16)],
        compiler_params=_params(("parallel",), tm * d * 48 + VMEM_SLACK),
        name="resid_norm",
    )(x, f, g_post, g_next)


def _mm_tiles(m):
    return _pick(m, (2080, 1040, 640, 320, 128))


def _mm_swiglu_kernel(x_ref, wg_ref, wu_ref, o_ref):
    x = x_ref[...]
    g = _dot(x, wg_ref[...].astype(BF16))
    u = _dot(x, wu_ref[...].astype(BF16))
    o_ref[...] = (_silu(g) * u).astype(o_ref.dtype)


def _ffn_up(h, w_up, l, s):
    m, d = h.shape
    tm, tn = _mm_tiles(m), 256
    noff = D_FF // tn
    return pl.pallas_call(
        _mm_swiglu_kernel,
        grid=(m // tm, D_FF // tn),
        in_specs=[pl.BlockSpec((tm, d), lambda i, j: (i, 0), pipeline_mode=pl.Buffered(1)),
                  pl.BlockSpec((None, None, d, tn), lambda i, j: (l, s, 0, j)),
                  pl.BlockSpec((None, None, d, tn), lambda i, j: (l, s, 0, j + noff))],
        out_specs=pl.BlockSpec((tm, tn), lambda i, j: (i, j)),
        out_shape=jax.ShapeDtypeStruct((m, D_FF), BF16),
        compiler_params=_params(("parallel", "arbitrary"),
                                tm * d * 2 + 4 * d * tn * 4 + 2 * d * tn * 2 + 6 * tm * tn * 4 + VMEM_SLACK),
        name="ffn_up",
    )(h, w_up, w_up)


def _mm_acc_kernel(x_ref, w_ref, o_ref):
    p = _dot(x_ref[...], w_ref[...].astype(BF16))
    k = pl.program_id(2)

    @pl.when(k == 0)
    def _():
        o_ref[...] = p

    @pl.when(k > 0)
    def _():
        o_ref[...] += p


def _ffn_down(a, w_down, l, s):
    m, kdim = a.shape
    n = w_down.shape[-1]
    tm, tn, tk = _mm_tiles(m), 1024, 256
    return pl.pallas_call(
        _mm_acc_kernel,
        grid=(m // tm, n // tn, kdim // tk),
        in_specs=[pl.BlockSpec((tm, tk), lambda i, j, k: (i, k)),
                  pl.BlockSpec((None, None, tk, tn), lambda i, j, k: (l, s, k, j))],
        out_specs=pl.BlockSpec((tm, tn), lambda i, j, k: (i, j)),
        out_shape=jax.ShapeDtypeStruct((m, n), F32),
        compiler_params=_params(("parallel", "parallel", "arbitrary"),
                                2 * tm * tn * 4 + tm * tn * 4 + 2 * tm * tk * 2 + 3 * tk * tn * 4 + VMEM_SLACK),
        name="ffn_down",
    )(a, w_down)


def _mm_plain_kernel(x_ref, w_ref, o_ref):
    o_ref[...] = _dot(x_ref[...], w_ref[...].astype(BF16)).astype(o_ref.dtype)


def _mm_plain(x, w, l, n, tn, name):
    m, d = x.shape
    tm = _mm_tiles(m)
    return pl.pallas_call(
        _mm_plain_kernel,
        grid=(m // tm, n // tn),
        in_specs=[pl.BlockSpec((tm, d), lambda i, j: (i, 0), pipeline_mode=pl.Buffered(1)),
                  pl.BlockSpec((None, d, tn), lambda i, j: (l, 0, j))],
        out_specs=pl.BlockSpec((tm, tn), lambda i, j: (i, j)),
        out_shape=jax.ShapeDtypeStruct((m, n), F32),
        compiler_params=_params(("parallel", "arbitrary"),
                                tm * d * 2 + 2 * d * tn * 4 + d * tn * 2 + 4 * tm * tn * 4 + VMEM_SLACK),
        name=name,
    )(x, w)


def _mm_gate_kernel(x_ref, w_ref, b_ref, o_ref):
    o_ref[...] = _sigmoid(_dot(x_ref[...], w_ref[...].astype(BF16)) + b_ref[...]).astype(o_ref.dtype)


def _gates(h, w_gate, b_gate3, l):
    m, d = h.shape
    n = w_gate.shape[-1]
    tm, tn = _mm_tiles(m), 256
    return pl.pallas_call(
        _mm_gate_kernel,
        grid=(m // tm, n // tn),
        in_specs=[pl.BlockSpec((tm, d), lambda i, j: (i, 0), pipeline_mode=pl.Buffered(1)),
                  pl.BlockSpec((None, d, tn), lambda i, j: (l, 0, j)),
                  pl.BlockSpec((None, 1, tn), lambda i, j: (l, 0, j))],
        out_specs=pl.BlockSpec((tm, tn), lambda i, j: (i, j)),
        out_shape=jax.ShapeDtypeStruct((m, n), BF16),
        compiler_params=_params(("parallel", "arbitrary"),
                                tm * d * 2 + 2 * d * tn * 4 + d * tn * 2 + 4 * tm * tn * 4 + VMEM_SLACK),
        name="gates",
    )(h, w_gate, b_gate3)


def _merge_kernel(ba_ref, bb_ref, bc_ref, bd_ref, w_ref, g_ref, o_ref, acc_ref, m_ref):
    kb = pl.program_id(2)
    w = w_ref[...].astype(BF16)

    @pl.when(kb == 0)
    def _():
        m_ref[...] = g_ref[...].astype(F32) * _dot(ba_ref[...], w)

    @pl.when(kb == 1)
    def _():
        m_ref[...] += g_ref[...].astype(F32) * _dot(bb_ref[...], w)

    @pl.when(kb == 2)
    def _():
        acc_ref[...] = _dot(bc_ref[...], w)

    @pl.when(kb == 3)
    def _():
        m_ref[...] += g_ref[...].astype(F32) * (acc_ref[...] + _dot(bc_ref[...], w))

    @pl.when(kb == 4)
    def _():
        o_ref[...] = (m_ref[...] + g_ref[...].astype(F32) * _dot(bd_ref[...], w)).astype(o_ref.dtype)


def _merge(br_a, br_b, br_c, br_d, w_branch, gates, l):
    m = br_a.shape[0]
    tk = BRANCH_WIDTH
    tm, tn = _pick(m, (1040, 640, 320, 128)), 512
    ncol = D_MODEL // tn
    return pl.pallas_call(
        _merge_kernel,
        grid=(m // tm, ncol, 5),
        in_specs=[pl.BlockSpec((tm, tk), lambda i, j, kb: (i, 0)),
                  pl.BlockSpec((tm, tk), lambda i, j, kb: (i, 0)),
                  pl.BlockSpec((tm, tk), lambda i, j, kb: (i, jnp.clip(kb - 2, 0, 1))),
                  pl.BlockSpec((tm, tk), lambda i, j, kb: (i, 0)),
                  pl.BlockSpec((None, tk, tn), lambda i, j, kb: (l, kb, j)),
                  pl.BlockSpec((tm, tn), lambda i, j, kb: (i, (kb - kb // 3) * ncol + j))],
        out_specs=pl.BlockSpec((tm, tn), lambda i, j, kb: (i, j)),
        out_shape=jax.ShapeDtypeStruct((m, D_MODEL), BF16),
        scratch_shapes=[pltpu.VMEM((tm, tn), F32), pltpu.VMEM((tm, tn), F32)],
        compiler_params=_params(("parallel", "parallel", "arbitrary"),
                                8 * tm * tk * 2 + 3 * tk * tn * 4 + 4 * tm * tn * 2 + 5 * tm * tn * 4 + VMEM_SLACK),
        name="merge",
    )(br_a, br_b, br_c, br_d, w_branch, gates)


def _log_sigmoid(x):
    return jnp.minimum(x, 0.0) - jnp.log1p(jnp.exp(-jnp.abs(x)))


def _gla_gate_kernel(h_ref, wlr_ref, wup_ref, b_ref, o_ref):
    glr = _dot(h_ref[...], wlr_ref[...].astype(BF16))
    pre = _dot(glr.astype(BF16), wup_ref[...].astype(BF16)) + b_ref[...]
    o_ref[...] = _log_sigmoid(pre) / GLA_TAU


def _gla_gate(h, w_lr, w_up, b_up):
    m, d = h.shape
    tm = _pick(m, (640, 320, 128))
    return pl.pallas_call(
        _gla_gate_kernel,
        grid=(m // tm,),
        in_specs=[pl.BlockSpec((tm, d), lambda i: (i, 0)),
                  pl.BlockSpec((d, LANES), lambda i: (0, 0)),
                  pl.BlockSpec((LANES, GLA_QK), lambda i: (0, 0)),
                  pl.BlockSpec((1, GLA_QK), lambda i: (0, 0))],
        out_specs=pl.BlockSpec((tm, GLA_QK), lambda i: (i, 0)),
        out_shape=jax.ShapeDtypeStruct((m, GLA_QK), F32),
        compiler_params=_params(("parallel",), 2 * tm * d * 2 + 3 * d * LANES * 4 + 6 * tm * GLA_QK * 4 + VMEM_SLACK),
        name="gla_gate",
    )(h, w_lr, w_up, b_up)


SCAN_BLOCK = 256


def _mid_rows(G, m):
    L, K = G.shape
    if 2 * m >= SUBLANES:
        G3 = G.reshape(L // (2 * m), 2 * m, K)
        return jnp.broadcast_to(G3[:, m - 1:m, :], G3.shape).reshape(L, K)
    assert m == 2
    G3 = G.reshape(L // SUBLANES, SUBLANES, K)
    sub = lax.broadcasted_iota(jnp.int32, G3.shape, 1)
    lo = jnp.broadcast_to(G3[:, 1:2, :], G3.shape)
    hi = jnp.broadcast_to(G3[:, 5:6, :], G3.shape)
    return jnp.where(sub < 4, lo, hi).reshape(L, K)


def _cumsum_rows(g):
    L = g.shape[0]
    tri = (lax.broadcasted_iota(jnp.int32, (L, L), 0) >= lax.broadcasted_iota(jnp.int32, (L, L), 1)).astype(BF16)
    hi = g.astype(BF16)
    r1 = g - hi.astype(F32)
    mid = r1.astype(BF16)
    lo = (r1 - mid.astype(F32)).astype(BF16)
    return _dot(tri, hi) + _dot(tri, mid) + _dot(tri, lo)


def _gated_block(q, k, g, v, st):
    L, K = q.shape
    G = _cumsum_rows(g)
    row = lax.broadcasted_iota(jnp.int32, (L, 1), 0)
    x = lax.broadcasted_iota(jnp.int32, (L, L), 0) ^ lax.broadcasted_iota(jnp.int32, (L, L), 1)
    scores = jnp.where(x == 0, _dot_nt(q.astype(BF16), k.astype(BF16)), 0.0)
    m = L // 2
    while m >= 1:
        second = (row & m) != 0
        if m == 1:
            e = jnp.where(second, g, 0.0)
        else:
            gm = _mid_rows(G, m)
            e = jnp.where(second, G - gm, gm - G)
        sc = jnp.exp(e)
        qs = jnp.where(second, q * sc, 0.0).astype(BF16)
        ks = jnp.where(second, 0.0, k * sc).astype(BF16)
        s = _dot_nt(qs, ks)
        scores = scores + (s if 2 * m == L else jnp.where(x < 2 * m, s, 0.0))
        m //= 2
    o = _dot(scores.astype(BF16), v.astype(BF16))
    o = o + _dot_nt((q * jnp.exp(G)).astype(BF16), st.astype(BF16))
    g_last = G[L - 1:L, :]
    k_out = (k * jnp.exp(g_last - G)).astype(BF16)
    st_new = st * jnp.exp(g_last) + _dot(v.T.astype(BF16), k_out)
    return o, st_new


def _head_rms(o, g):
    return o * lax.rsqrt(jnp.mean(o * o, axis=-1, keepdims=True) + EPS) * g


def _hgrn_lb(logits, l):
    e = jnp.exp(logits - jnp.max(logits, axis=0, keepdims=True))
    p = e / jnp.sum(e, axis=0, keepdims=True)
    lb = jnp.zeros_like(p[0:1])
    for i in range(1, l + 1):
        lb = lb + p[i:i + 1]
    return lb


def _hgrn_qkg(hq, hf, lb):
    sig = _sigmoid(hf)
    q = _silu(hq)
    g = jnp.log(lb + (1.0 - lb) * sig)
    k = (1.0 - lb) * _sigmoid(-hf)
    return q, k, g


def _hgrn_prompt_kernel(hq_ref, hf_ref, hi_ref, hg_ref, lbl_ref, gn_ref, br_ref, s_ref, st_ref, *, l):
    t = pl.program_id(2)

    @pl.when(t == 0)
    def _():
        st_ref[...] = jnp.zeros_like(st_ref)

    q, k, g = _hgrn_qkg(hq_ref[...], hf_ref[...], _hgrn_lb(lbl_ref[...], l))
    o, st_new = _gated_block(q, k, g, hi_ref[...], st_ref[...])
    st_ref[...] = st_new
    br_ref[...] = (_head_rms(o, gn_ref[...]) * _silu(hg_ref[...])).astype(br_ref.dtype)

    @pl.when(t == pl.num_programs(2) - 1)
    def _():
        s_ref[...] = st_new.T


def _gla_prompt_kernel(gq_ref, gk_ref, gv_ref, gg_ref, la_ref, gn_ref, br_ref, s_ref, st_ref):
    t = pl.program_id(2)

    @pl.when(t == 0)
    def _():
        st_ref[...] = jnp.zeros_like(st_ref)

    o, st_new = _gated_block(gq_ref[...] * GLA_DK ** -0.5, gk_ref[...], la_ref[...], gv_ref[...], st_ref[...])
    st_ref[...] = st_new
    br_ref[...] = (_head_rms(o, gn_ref[...]) * _silu(gg_ref[...])).astype(br_ref.dtype)

    @pl.when(t == pl.num_programs(2) - 1)
    def _():
        s_ref[...] = st_new.T


def _scan_vmem(L, K, V):
    return (24 * L * L + 40 * L * K + 16 * L * V + 8 * K * V) * 4 + (8 << 20)


def _hgrn_prompt(z, lb_logits, norm_g3, l, bp, t_len):
    L = min(SCAN_BLOCK, t_len)
    nt = t_len // L
    K = V = HG_DK

    def col(off, w):
        return lambda b, h, t: (b * nt + t, off // w + h)

    return pl.pallas_call(
        functools.partial(_hgrn_prompt_kernel, l=l),
        grid=(bp, HG_HEADS, nt),
        in_specs=[pl.BlockSpec((L, K), col(OFF_HQ, K)), pl.BlockSpec((L, K), col(OFF_HF, K)),
                  pl.BlockSpec((L, V), col(OFF_HI, V)), pl.BlockSpec((L, V), col(OFF_HG, V)),
                  pl.BlockSpec((DEPTH, K), lambda b, h, t: (0, h)),
                  pl.BlockSpec((None, 1, V), lambda b, h, t: (l, 0, h))],
        out_specs=[pl.BlockSpec((L, V), lambda b, h, t: (b * nt + t, h)),
                   pl.BlockSpec((None, None, K, V), lambda b, h, t: (b, h, 0, 0))],
        out_shape=[jax.ShapeDtypeStruct((bp * t_len, HG_HEADS * V), BF16),
                   jax.ShapeDtypeStruct((bp, HG_HEADS, K, V), F32)],
        scratch_shapes=[pltpu.VMEM((V, K), F32)],
        compiler_params=_params(("parallel", "parallel", "arbitrary"), _scan_vmem(L, K, V)),
        name="hgrn_prompt",
    )(z, z, z, z, lb_logits, norm_g3)


def _gla_prompt(z, log_alpha, norm_g3, l, bp, t_len):
    L = min(SCAN_BLOCK, t_len)
    nt = t_len // L
    K, V = GLA_DK, GLA_DV

    def col(off, w):
        return lambda b, h, t: (b * nt + t, off // w + h)

    return pl.pallas_call(
        _gla_prompt_kernel,
        grid=(bp, GLA_HEADS, nt),
        in_specs=[pl.BlockSpec((L, K), col(OFF_GQ, K)), pl.BlockSpec((L, K), col(OFF_GK, K)),
                  pl.BlockSpec((L, V), col(OFF_GV, V)), pl.BlockSpec((L, V), col(OFF_GG, V)),
                  pl.BlockSpec((L, K), col(0, K)),
                  pl.BlockSpec((None, 1, V), lambda b, h, t: (l, 0, h))],
        out_specs=[pl.BlockSpec((L, V), lambda b, h, t: (b * nt + t, h)),
                   pl.BlockSpec((None, None, K, V), lambda b, h, t: (b, h, 0, 0))],
        out_shape=[jax.ShapeDtypeStruct((bp * t_len, GLA_HEADS * V), BF16),
                   jax.ShapeDtypeStruct((bp, GLA_HEADS, K, V), F32)],
        scratch_shapes=[pltpu.VMEM((V, K), F32)],
        compiler_params=_params(("parallel", "parallel", "arbitrary"), _scan_vmem(L, K, V)),
        name="gla_prompt",
    )(z, z, z, z, log_alpha, norm_g3)


def _rotate(x, cos, sin):
    half = x.shape[-1] // 2
    x1, x2 = x[:, :half], x[:, half:]
    return jnp.concatenate([x1 * cos - x2 * sin, x1 * sin + x2 * cos], axis=-1)


def _group_norm(o, g):
    c = o - jnp.mean(o, axis=-1, keepdims=True)
    return c * lax.rsqrt(jnp.mean(c * c, axis=-1, keepdims=True) + EPS) * g


def _ret_prompt_kernel(lg_ref, rq_ref, rk_ref, rv_ref, rg_ref, cos_ref, sin_ref, gn_ref, br_ref, s_ref, st_ref):
    h = pl.program_id(1)
    t = pl.program_id(2)
    lg = lg_ref[h]

    @pl.when(t == 0)
    def _():
        st_ref[...] = jnp.zeros_like(st_ref)

    cos, sin = cos_ref[...], sin_ref[...]
    q = _rotate(rq_ref[...], cos, sin)
    k = _rotate(rk_ref[...], cos, sin) * RET_DK ** -0.5
    v = rv_ref[...].astype(BF16)
    L, K = q.shape
    dist = (lax.broadcasted_iota(jnp.int32, (L, L), 0) - lax.broadcasted_iota(jnp.int32, (L, L), 1)).astype(F32)
    causal = dist >= 0
    decay = jnp.where(causal, jnp.exp(jnp.where(causal, dist * lg, 0.0)), 0.0)
    scores = _dot_nt(q.astype(BF16), k.astype(BF16)) * decay
    o = _dot(scores.astype(BF16), v)
    rowf = lax.broadcasted_iota(jnp.int32, (L, K), 0).astype(F32)
    st = st_ref[...]
    o = o + _dot((q * jnp.exp((rowf + 1.0) * lg)).astype(BF16), st.astype(BF16))
    k_out = k * jnp.exp((L - 1.0 - rowf) * lg)
    st_new = st * jnp.exp(jnp.full((1, 1), L, F32) * lg) + _dot(k_out.T.astype(BF16), v)
    st_ref[...] = st_new
    br_ref[...] = (_group_norm(o, gn_ref[...]) * _silu(rg_ref[...])).astype(br_ref.dtype)

    @pl.when(t == pl.num_programs(2) - 1)
    def _():
        s_ref[...] = st_new


def _ret_prompt(z, log_gamma, cos, sin, norm_g3, l, bp, t_len):
    L = min(SCAN_BLOCK, t_len)
    nt = t_len // L
    K, V = RET_DK, RET_DV

    def col(off, w):
        return lambda b, h, t: (b * nt + t, off // w + h)

    return pl.pallas_call(
        _ret_prompt_kernel,
        grid=(bp, RET_HEADS, nt),
        in_specs=[pl.BlockSpec(memory_space=pltpu.SMEM),
                  pl.BlockSpec((L, K), col(OFF_RQ, K)), pl.BlockSpec((L, K), col(OFF_RK, K)),
                  pl.BlockSpec((L, V), col(OFF_RV, V)), pl.BlockSpec((L, V), col(OFF_RG, V)),
                  pl.BlockSpec((L, K // 2), lambda b, h, t: (t, 0)),
                  pl.BlockSpec((L, K // 2), lambda b, h, t: (t, 0)),
                  pl.BlockSpec((None, 1, V), lambda b, h, t: (l, 0, h))],
        out_specs=[pl.BlockSpec((L, V), lambda b, h, t: (b * nt + t, h)),
                   pl.BlockSpec((None, None, K, V), lambda b, h, t: (b, h, 0, 0))],
        out_shape=[jax.ShapeDtypeStruct((bp * t_len, RET_HEADS * V), BF16),
                   jax.ShapeDtypeStruct((bp, RET_HEADS, K, V), F32)],
        scratch_shapes=[pltpu.VMEM((K, V), F32)],
        compiler_params=_params(("parallel", "parallel", "arbitrary"), _scan_vmem(L, K, V)),
        name="ret_prompt",
    )(log_gamma, z, z, z, z, cos, sin, norm_g3)


LRU_COLS = 256
CARRY_ROWS = SUBLANES


def _softplus(x):
    return jnp.maximum(x, 0.0) + jnp.log1p(jnp.exp(-jnp.abs(x)))


def _lru_gates(u, wa_ref, ba, wx_ref, bx, lam):
    nb = u.shape[1] // LRU_BW
    ra, rx = [], []
    for n in range(nb):
        un = u[:, n * LRU_BW:(n + 1) * LRU_BW].astype(BF16)
        ra.append(_dot(un, wa_ref[n].astype(BF16)))
        rx.append(_dot(un, wx_ref[n].astype(BF16)))
    r = _sigmoid(jnp.concatenate(ra, axis=1) + ba)
    i_g = _sigmoid(jnp.concatenate(rx, axis=1) + bx)
    log_a = -LRU_C * r * _softplus(-lam)
    a = jnp.exp(log_a)
    th = jnp.tanh(log_a)
    mult = jnp.sqrt(jnp.maximum(-2.0 * th / (1.0 - th), 0.0))
    return a, mult, i_g


def _lru_prompt_kernel(lx_ref, ly_ref, cw_ref, cb_ref, wa_ref, ba_ref, wx_ref, bx_ref, lam_ref,
                       br_ref, h_ref, c_ref, xbuf_ref, hprev_ref):
    t = pl.program_id(2)
    R = lx_ref.shape[0]

    @pl.when(t == 0)
    def _():
        xbuf_ref[0:CARRY_ROWS, :] = jnp.zeros((CARRY_ROWS, xbuf_ref.shape[1]), F32)
        hprev_ref[...] = jnp.zeros_like(hprev_ref)

    x = lx_ref[...]
    xbuf_ref[CARRY_ROWS:CARRY_ROWS + R, :] = x
    cw = cw_ref[...]
    u = cb_ref[...] + xbuf_ref[CARRY_ROWS - 3:CARRY_ROWS - 3 + R, :] * cw[0:1]
    u = u + xbuf_ref[CARRY_ROWS - 2:CARRY_ROWS - 2 + R, :] * cw[1:2]
    u = u + xbuf_ref[CARRY_ROWS - 1:CARRY_ROWS - 1 + R, :] * cw[2:3]
    u = u + x * cw[3:4]
    a, mult, i_g = _lru_gates(u, wa_ref, ba_ref[...], wx_ref, bx_ref[...], lam_ref[...])
    row = lax.broadcasted_iota(jnp.int32, (R, 1), 0)
    mult = jnp.where(row == jnp.where(t == 0, 0, -1), 1.0, mult)
    b = mult * i_g * u
    d = 1
    while d < R:
        valid = row >= d
        a_sh = pltpu.roll(a, d, 0)
        b_sh = pltpu.roll(b, d, 0)
        b = jnp.where(valid, a * b_sh + b, b)
        a = jnp.where(valid, a * a_sh, a)
        d *= 2
    hs = a * hprev_ref[0:1, :] + b
    br_ref[...] = (hs * jax.nn.gelu(ly_ref[...], approximate=True)).astype(br_ref.dtype)
    xbuf_ref[0:CARRY_ROWS, :] = x[R - CARRY_ROWS:R, :]
    hprev_ref[...] = jnp.broadcast_to(hs[R - 1:R, :], hprev_ref.shape)

    @pl.when(t == pl.num_programs(2) - 1)
    def _():
        h_ref[...] = hs[R - 1:R, :]
        c_ref[...] = x[R - (CONV_W - 1):R, :]


def _lru_prompt(z, P, l, bp, t_len):
    R = min(SCAN_BLOCK, t_len)
    nt = t_len // R
    C = LRU_COLS
    nc = LRU_WIDTH // C
    nb = C // LRU_BW
    vec = lambda: pl.BlockSpec((None, 1, C), lambda b, c, t: (l, 0, c))
    wblk = lambda: pl.BlockSpec((None, nb, LRU_BW, LRU_BW), lambda b, c, t: (l, c, 0, 0))
    return pl.pallas_call(
        _lru_prompt_kernel,
        grid=(bp, nc, nt),
        in_specs=[pl.BlockSpec((R, C), lambda b, c, t: (b * nt + t, OFF_LX // C + c)),
                  pl.BlockSpec((R, C), lambda b, c, t: (b * nt + t, OFF_LY // C + c)),
                  pl.BlockSpec((None, CONV_W, C), lambda b, c, t: (l, 0, c)),
                  vec(), wblk(), vec(), wblk(), vec(), vec()],
        out_specs=[pl.BlockSpec((R, C), lambda b, c, t: (b * nt + t, c)),
                   pl.BlockSpec((None, 1, C), lambda b, c, t: (b, 0, c)),
                   pl.BlockSpec((None, CONV_W - 1, C), lambda b, c, t: (b, 0, c))],
        out_shape=[jax.ShapeDtypeStruct((bp * t_len, LRU_WIDTH), BF16),
                   jax.ShapeDtypeStruct((bp, 1, LRU_WIDTH), F32),
                   jax.ShapeDtypeStruct((bp, CONV_W - 1, LRU_WIDTH), F32)],
        scratch_shapes=[pltpu.VMEM((R + CARRY_ROWS, C), F32), pltpu.VMEM((CARRY_ROWS, C), F32)],
        compiler_params=_params(("parallel", "parallel", "arbitrary"), 40 * R * C * 4 + (8 << 20)),
        name="lru_prompt",
    )(z, z, P['rglru_conv_w'], P['conv_b3'], P['rglru_w_a'], P['b_a3'], P['rglru_w_x'], P['b_x3'], P['lambda3'])


def _columns(rows):
    bb, K = rows.shape
    pad = jnp.concatenate([rows, jnp.zeros((LANES - bb, K), F32)], axis=0) if bb < LANES else rows
    return jnp.concatenate([pad[:, c * LANES:(c + 1) * LANES].T for c in range(K // LANES)], axis=0)


def _state_step(s_ref, so_ref, q, k, v, f_cols):
    bb, K = q.shape
    V = v.shape[1]
    k_cols = _columns(k)
    outs = []
    for i in range(bb):
        s = s_ref[i]
        kc = jnp.broadcast_to(k_cols[:, i:i + 1], (K, V))
        if f_cols.shape[0] == 1:
            s_new = s * f_cols + kc * v[i:i + 1, :]
        else:
            s_new = s * jnp.broadcast_to(f_cols[:, i:i + 1], (K, V)) + kc * v[i:i + 1, :]
        so_ref[i] = s_new
        qi = jnp.broadcast_to(q[i:i + 1, :], (SUBLANES, K)).astype(BF16)
        outs.append(_dot(qi, s_new.astype(BF16))[0:1, :])
    return jnp.concatenate(outs, axis=0)


def _hgrn_step_kernel(hq_ref, hf_ref, hi_ref, hg_ref, lbl_ref, gn_ref, s_ref, br_ref, so_ref, *, l):
    lb = _hgrn_lb(lbl_ref[...], l)
    q, k, g = _hgrn_qkg(hq_ref[...], hf_ref[...], lb)
    o = _state_step(s_ref, so_ref, q, k, hi_ref[...], _columns(jnp.exp(g)))
    br_ref[...] = (_head_rms(o, gn_ref[...]) * _silu(hg_ref[...])).astype(br_ref.dtype)


def _gla_step_kernel(gq_ref, gk_ref, gv_ref, gg_ref, la_ref, gn_ref, s_ref, br_ref, so_ref):
    o = _state_step(s_ref, so_ref, gq_ref[...] * GLA_DK ** -0.5, gk_ref[...], gv_ref[...],
                    _columns(jnp.exp(la_ref[...])))
    br_ref[...] = (_head_rms(o, gn_ref[...]) * _silu(gg_ref[...])).astype(br_ref.dtype)


def _ret_step_kernel(lg_ref, rq_ref, rk_ref, rv_ref, rg_ref, cos_ref, sin_ref, gn_ref, s_ref, br_ref, so_ref):
    lg = lg_ref[pl.program_id(0)]
    cos, sin = cos_ref[...], sin_ref[...]
    q = _rotate(rq_ref[...], cos, sin)
    k = _rotate(rk_ref[...], cos, sin) * RET_DK ** -0.5
    gamma = jnp.exp(jnp.full((1, 1), 1.0, F32) * lg)
    o = _state_step(s_ref, so_ref, q, k, rv_ref[...], gamma)
    br_ref[...] = (_group_norm(o, gn_ref[...]) * _silu(rg_ref[...])).astype(br_ref.dtype)


def _step_call(kernel, z, extra, extra_specs, state, l, row0, offs, K, V, H, bb, name, smem=None):
    bs = state.shape[1]
    rb = row0 // bb

    def col(off, w):
        return pl.BlockSpec((bb, w), lambda h, b: (rb + b, off // w + h))

    in_specs = [col(offs[0], K), col(offs[1], K), col(offs[2], V), col(offs[3], V)] + extra_specs
    in_specs.append(pl.BlockSpec((None, bb, None, K, V), lambda h, b: (l, b, h, 0, 0)))
    args = [z, z, z, z] + extra + [state]
    if smem is not None:
        in_specs = [pl.BlockSpec(memory_space=pltpu.SMEM)] + in_specs
        args = [smem] + args
    return pl.pallas_call(
        kernel,
        grid=(H, bs // bb),
        in_specs=in_specs,
        out_specs=[pl.BlockSpec((bb, V), lambda h, b: (b, h)),
                   pl.BlockSpec((bb, None, K, V), lambda h, b: (b, h, 0, 0))],
        out_shape=[jax.ShapeDtypeStruct((bs, H * V), BF16),
                   jax.ShapeDtypeStruct((bs, H, K, V), F32)],
        compiler_params=_params(("parallel", "parallel"), 6 * bb * K * V * 4 + (8 << 20)),
        name=name,
    )(*args)


def _lru_step_kernel(lx_ref, ly_ref, cw_ref, cb_ref, wa_ref, ba_ref, wx_ref, bx_ref, lam_ref, c_ref, h_ref,
                     br_ref, ho_ref, co_ref):
    x = lx_ref[...]
    cw = cw_ref[...]
    u = cb_ref[...] + c_ref[0] * cw[0:1]
    u = u + c_ref[1] * cw[1:2]
    u = u + c_ref[2] * cw[2:3]
    u = u + x * cw[3:4]
    a, mult, i_g = _lru_gates(u, wa_ref, ba_ref[...], wx_ref, bx_ref[...], lam_ref[...])
    hs = a * h_ref[...] + mult * i_g * u
    br_ref[...] = (hs * jax.nn.gelu(ly_ref[...], approximate=True)).astype(br_ref.dtype)
    ho_ref[...] = hs
    co_ref[0] = c_ref[1]
    co_ref[1] = c_ref[2]
    co_ref[2] = x


def _lru_step(z, P, conv_state_t, h_state, l, row0):
    bs = h_state.shape[1]
    W = LRU_WIDTH
    vec = lambda: pl.BlockSpec((None, 1, W), lambda i: (l, 0, 0))
    wblk = lambda: pl.BlockSpec((None, LRU_BLOCKS, LRU_BW, LRU_BW), lambda i: (l, 0, 0, 0))
    return pl.pallas_call(
        _lru_step_kernel,
        grid=(1,),
        in_specs=[pl.BlockSpec((bs, W), lambda i: (row0 // bs, OFF_LX // W)),
                  pl.BlockSpec((bs, W), lambda i: (row0 // bs, OFF_LY // W)),
                  pl.BlockSpec((None, CONV_W, W), lambda i: (l, 0, 0)),
                  vec(), wblk(), vec(), wblk(), vec(), vec(),
                  pl.BlockSpec((None, CONV_W - 1, bs, W), lambda i: (l, 0, 0, 0)),
                  pl.BlockSpec((None, bs, W), lambda i: (l, 0, 0))],
        out_specs=[pl.BlockSpec((bs, W), lambda i: (0, 0)),
                   pl.BlockSpec((bs, W), lambda i: (0, 0)),
                   pl.BlockSpec((CONV_W - 1, bs, W), lambda i: (0, 0, 0))],
        out_shape=[jax.ShapeDtypeStruct((bs, W), BF16), jax.ShapeDtypeStruct((bs, W), F32),
                   jax.ShapeDtypeStruct((CONV_W - 1, bs, W), F32)],
        compiler_params=_params(("arbitrary",), 64 * bs * W * 4 + (8 << 20)),
        name="lru_step",
    )(z, z, P['rglru_conv_w'], P['conv_b3'], P['rglru_w_a'], P['b_a3'], P['rglru_w_x'], P['b_x3'], P['lambda3'],
      conv_state_t, h_state)


def _rope_tables(pos):
    half = RET_DK // 2
    freqs = jnp.exp(-math.log(ROPE_BASE) * jnp.arange(half, dtype=F32) / half)
    ang = pos.astype(F32)[:, None] * freqs[None]
    return jnp.cos(ang), jnp.sin(ang)


def _mixer(h, st, l, P, bp, t_len, bs):
    mp = bp * t_len
    z = _mm_plain(h, P['w_in'], l, D_Z, 256, "w_in")
    log_alpha = _gla_gate(h, P['w_lr'][l], P['w_up_pad'][l], P['b_up3'][l])
    gates = _gates(h, P['w_gate'], P['b_gate3'], l)

    a_p, s_hg_p = _hgrn_prompt(z, P['hgrn_lb_logits'], P['hgrn_g3'], l, bp, t_len)
    b_p, s_lh_p, s_lc_p = _lru_prompt(z, P, l, bp, t_len)
    c_p, s_rt_p = _ret_prompt(z, P['log_gamma'], P['cos_p'], P['sin_p'], P['ret_g3'], l, bp, t_len)
    d_p, s_gl_p = _gla_prompt(z, log_alpha, P['gla_g3'], l, bp, t_len)

    s_hg, s_lh, s_lc, s_rt, s_gl = st
    bb = 16
    a_s, s_hg_s = _step_call(
        functools.partial(_hgrn_step_kernel, l=l), z,
        [P['hgrn_lb_logits'], P['hgrn_g3']],
        [pl.BlockSpec((DEPTH, HG_DK), lambda h_, b: (0, h_)),
         pl.BlockSpec((None, 1, HG_DK), lambda h_, b: (l, 0, h_))],
        s_hg, l, mp, (OFF_HQ, OFF_HF, OFF_HI, OFF_HG), HG_DK, HG_DK, HG_HEADS, bb, "hgrn_step")
    b_s, s_lh_s, s_lc_s = _lru_step(z, P, s_lc, s_lh, l, mp)
    s_lc_s = jnp.swapaxes(s_lc_s, 0, 1)
    c_s, s_rt_s = _step_call(
        _ret_step_kernel, z,
        [P['cos_s'], P['sin_s'], P['ret_g3']],
        [pl.BlockSpec((1, RET_DK // 2), lambda h_, b: (0, 0)),
         pl.BlockSpec((1, RET_DK // 2), lambda h_, b: (0, 0)),
         pl.BlockSpec((None, 1, RET_DV), lambda h_, b: (l, 0, h_))],
        s_rt, l, mp, (OFF_RQ, OFF_RK, OFF_RV, OFF_RG), RET_DK, RET_DV, RET_HEADS, 8, "ret_step",
        smem=P['log_gamma'])
    d_s, s_gl_s = _step_call(
        _gla_step_kernel, z,
        [log_alpha, P['gla_g3']],
        [pl.BlockSpec((bb, GLA_DK), lambda h_, b: (mp // bb + b, h_)),
         pl.BlockSpec((None, 1, GLA_DV), lambda h_, b: (l, 0, h_))],
        s_gl, l, mp, (OFF_GQ, OFF_GK, OFF_GV, OFF_GG), GLA_DK, GLA_DV, GLA_HEADS, bb, "gla_step")

    cat = lambda p, s: jnp.concatenate([p, s], axis=0)
    merged = _merge(cat(a_p, a_s), cat(b_p, b_s), cat(c_p, c_s), cat(d_p, d_s), P['w_branch'], gates, l)
    out = _mm_plain(merged, P['w_out'], l, D_MODEL, 256, "w_out")
    new_p = (s_hg_p, s_lh_p.reshape(bp, LRU_WIDTH), s_lc_p, s_rt_p, s_gl_p)
    new_s = (s_hg_s, s_lh_s, s_lc_s, s_rt_s, s_gl_s)
    return out, new_p, new_s


def _layer(x, h, st, l, P, bp, t_len, bs, last):
    g = P['norm_g']
    f1 = _ffn_down(_ffn_up(h, P['w_ff_up'], l, 0), P['w_ff_down'], l, 0)
    x, h = _resid_norm(x, f1, g[l, 1:2], 0.5, g[l, 2:3])
    m, new_p, new_s = _mixer(h, st, l, P, bp, t_len, bs)
    x, h = _resid_norm(x, m, g[l, 3:4], 1.0, g[l, 4:5])
    f2 = _ffn_down(_ffn_up(h, P['w_ff_up'], l, 1), P['w_ff_down'], l, 1)
    x, h = _resid_norm(x, f2, g[l, 5:6], 0.5, None if last else g[l + 1, 0:1])
    return x, h, new_p, new_s


def kernel(x_prompt, x_sample, state_hgrn, state_rglru_h, state_rglru_conv, state_retention, state_gla,
           norm_g, w_ff_up, w_ff_down, w_in, hgrn_lb_logits, hgrn_norm_g,
           rglru_conv_w, rglru_conv_b, rglru_w_a, rglru_b_a, rglru_w_x, rglru_b_x, rglru_lambda,
           ret_norm_g, gla_w_up, gla_b_up, gla_norm_g, w_branch, w_gate, b_gate, w_out):
    bp, t_len, d = x_prompt.shape
    bs = x_sample.shape[0]
    depth = norm_g.shape[0]
    P = _prep(t_len, norm_g, w_ff_up, w_ff_down, w_in, hgrn_lb_logits, hgrn_norm_g,
              rglru_conv_w, rglru_conv_b, rglru_w_a, rglru_b_a, rglru_w_x, rglru_b_x, rglru_lambda,
              ret_norm_g, gla_w_up, gla_b_up, gla_norm_g, w_branch, w_gate, b_gate, w_out)
    st = (state_hgrn, state_rglru_h, jnp.swapaxes(state_rglru_conv, 1, 2), state_retention, state_gla)

    x = jnp.concatenate([x_prompt.reshape(bp * t_len, d), x_sample.reshape(bs, d)], axis=0)
    h = _rmsnorm_rows(x, norm_g[0, 0:1])
    new_p, new_s = [], []
    for l in range(depth):
        x, h, stp, sts = _layer(x, h, st, l, P, bp, t_len, bs, l == depth - 1)
        new_p.append(stp)
        new_s.append(sts)
    yp = x[:bp * t_len].reshape(bp, t_len, d)
    ys = x[bp * t_len:].reshape(bs, 1, d)
    sp = [jnp.stack([s[i] for s in new_p]) for i in range(5)]
    ss = [jnp.stack([s[i] for s in new_s]) for i in range(5)]
    return (yp, ys, sp[0], sp[1], sp[2], sp[3], sp[4], ss[0], ss[1], ss[2], ss[3], ss[4])


def _prep(t_len, norm_g, w_ff_up, w_ff_down, w_in, hgrn_lb_logits, hgrn_norm_g,
          rglru_conv_w, rglru_conv_b, rglru_w_a, rglru_b_a, rglru_w_x, rglru_b_x, rglru_lambda,
          ret_norm_g, gla_w_up, gla_b_up, gla_norm_g, w_branch, w_gate, b_gate, w_out):
    row3 = lambda a: a.reshape(a.shape[0], 1, a.shape[1])
    cos_p, sin_p = _rope_tables(jnp.arange(t_len, dtype=jnp.int32))
    cos_s, sin_s = _rope_tables(PAST_LEN + jnp.arange(1, dtype=jnp.int32))
    P = dict(norm_g=norm_g, w_ff_up=w_ff_up, w_ff_down=w_ff_down, w_in=w_in,
             hgrn_lb_logits=hgrn_lb_logits, hgrn_g3=row3(hgrn_norm_g),
             rglru_conv_w=rglru_conv_w, conv_b3=row3(rglru_conv_b), rglru_w_a=rglru_w_a, b_a3=row3(rglru_b_a),
             rglru_w_x=rglru_w_x, b_x3=row3(rglru_b_x), lambda3=row3(rglru_lambda),
             ret_g3=row3(ret_norm_g), gla_g3=row3(gla_norm_g), b_up3=row3(gla_b_up),
             w_lr=jnp.pad(w_in[:, :, OFF_GLR:], ((0, 0), (0, 0), (0, LANES - GLA_RANK))),
             w_up_pad=jnp.pad(gla_w_up, ((0, 0), (0, LANES - GLA_RANK), (0, 0))),
             w_branch=w_branch, w_gate=w_gate, b_gate3=row3(b_gate), w_out=w_out,
             log_gamma=jnp.log1p(-jnp.exp2(-5.0 - jnp.arange(RET_HEADS, dtype=F32))),
             cos_p=cos_p, sin_p=sin_p, cos_s=cos_s, sin_s=sin_s)
    return P
```

```python
import functools
import math

import jax
import jax.numpy as jnp
from jax import lax
from jax.experimental import pallas as pl
from jax.experimental.pallas import tpu as pltpu

F32 = jnp.float32
BF16 = jnp.bfloat16

D_MODEL = 4096
DEPTH = 4
PAST_LEN = 16384
N_BRANCH = 4
BRANCH_WIDTH = D_MODEL // 4
HG_HEADS = 8
HG_DK = BRANCH_WIDTH // HG_HEADS
LRU_WIDTH = BRANCH_WIDTH
LRU_BLOCKS = 8
LRU_BW = LRU_WIDTH // LRU_BLOCKS
CONV_W = 4
LRU_C = 8.0
RET_HEADS = 4
RET_DK = BRANCH_WIDTH // RET_HEADS
RET_DV = 2 * RET_DK
ROPE_BASE = 10000.0
GLA_HEADS = 4
GLA_DK = BRANCH_WIDTH // 2 // GLA_HEADS
GLA_DV = BRANCH_WIDTH // GLA_HEADS
GLA_QK = GLA_HEADS * GLA_DK
GLA_RANK = 16
GLA_TAU = 16.0
D_FF = 256 * ((8 * D_MODEL // 3 + 255) // 256)
EPS = 1e-6

OFF_HQ, OFF_HF, OFF_HI, OFF_HG = 0, 1024, 2048, 3072
OFF_LX, OFF_LY = 4096, 5120
OFF_RQ, OFF_RK, OFF_RV, OFF_RG = 6144, 7168, 8192, 10240
OFF_GQ, OFF_GK, OFF_GV, OFF_GG = 12288, 12800, 13312, 14336
OFF_GLR = 15360
D_Z = OFF_GLR

LANES = 128
SUBLANES = 8
VMEM_LIMIT_CAP = 58 * 1024 * 1024
VMEM_SLACK = 12 * 1024 * 1024


def _params(semantics, vmem_bytes):
    return pltpu.CompilerParams(dimension_semantics=semantics,
                                vmem_limit_bytes=int(min(max(vmem_bytes, 16 << 20), VMEM_LIMIT_CAP)))


def _pick(n, candidates):
    for c in candidates:
        if n % c == 0:
            return c
    raise ValueError(f"no tile of {candidates} divides {n}")


def _sigmoid(x):
    return 1.0 / (1.0 + jnp.exp(-x))


def _silu(x):
    return x * _sigmoid(x)


def _dot(a, b):
    return jnp.dot(a, b, preferred_element_type=F32)


def _dot_nt(a, b):
    return lax.dot_general(a, b, (((1,), (1,)), ((), ())), preferred_element_type=F32)


def _rms(x, g):
    return x * lax.rsqrt(jnp.mean(x * x, axis=-1, keepdims=True) + EPS) * g


def _norm_kernel(x_ref, g_ref, h_ref):
    h_ref[...] = _rms(x_ref[...], g_ref[...]).astype(h_ref.dtype)


def _rmsnorm_rows(x, g_row):
    m, d = x.shape
    tm = _pick(m, (320, 160, 128, 64, 32, 16))
    return pl.pallas_call(
        _norm_kernel,
        grid=(m // tm,),
        in_specs=[pl.BlockSpec((tm, d), lambda i: (i, 0)), pl.BlockSpec((1, d), lambda i: (0, 0))],
        out_specs=pl.BlockSpec((tm, d), lambda i: (i, 0)),
        out_shape=jax.ShapeDtypeStruct((m, d), BF16),
        compiler_params=_params(("parallel",), tm * d * 24 + VMEM_SLACK),
        name="rmsnorm",
    )(x, g_row)


def _resid_norm_kernel(x_ref, f_ref, gp_ref, gn_ref, xo_ref, h_ref, *, scale):
    xn = x_ref[...] + scale * _rms(f_ref[...], gp_ref[...])
    xo_ref[...] = xn
    h_ref[...] = _rms(xn, gn_ref[...]).astype(h_ref.dtype)


def _resid_kernel(x_ref, f_ref, gp_ref, xo_ref, *, scale):
    xo_ref[...] = x_ref[...] + scale * _rms(f_ref[...], gp_ref[...])


def _resid_norm(x, f, g_post, scale, g_next):
    m, d = x.shape
    tm = _pick(m, (160, 128, 64, 32, 16))
    row = pl.BlockSpec((tm, d), lambda i: (i, 0))
    vec = pl.BlockSpec((1, d), lambda i: (0, 0))
    if g_next is None:
        return pl.pallas_call(
            functools.partial(_resid_kernel, scale=scale),
            grid=(m // tm,), in_specs=[row, row, vec], out_specs=row,
            out_shape=jax.ShapeDtypeStruct((m, d), F32),
            compiler_params=_params(("parallel",), tm * d * 40 + VMEM_SLACK),
            name="resid",
        )(x, f, g_post), None
    return pl.pallas_call(
        functools.partial(_resid_norm_kernel, scale=scale),
        grid=(m // tm,), in_specs=[row, row, vec, vec], out_specs=[row, row],
        out_shape=[jax.ShapeDtypeStruct((m, d), F32), jax.ShapeDtypeStruct((m, d), BF16)],
        compiler_params=_params(("parallel",), tm * d * 48 + VMEM_SLACK),
        name="resid_norm",
    )(x, f, g_post, g_next)


def _mm_tiles(m):
    return _pick(m, (2080, 1040, 640, 320, 128))


def _mm_swiglu_kernel(x_ref, wg_ref, wu_ref, o_ref):
    x = x_ref[...]
    g = _dot(x, wg_ref[...].astype(BF16))
    u = _dot(x, wu_ref[...].astype(BF16))
    o_ref[...] = (_silu(g) * u).astype(o_ref.dtype)


def _ffn_up(h, w_up, l, s):
    m, d = h.shape
    tm, tn = _mm_tiles(m), 256
    noff = D_FF // tn
    return pl.pallas_call(
        _mm_swiglu_kernel,
        grid=(m // tm, D_FF // tn),
        in_specs=[pl.BlockSpec((tm, d), lambda i, j: (i, 0), pipeline_mode=pl.Buffered(1)),
                  pl.BlockSpec((None, None, d, tn), lambda i, j: (l, s, 0, j)),
                  pl.BlockSpec((None, None, d, tn), lambda i, j: (l, s, 0, j + noff))],
        out_specs=pl.BlockSpec((tm, tn), lambda i, j: (i, j)),
        out_shape=jax.ShapeDtypeStruct((m, D_FF), BF16),
        compiler_params=_params(("parallel", "arbitrary"),
                                tm * d * 2 + 4 * d * tn * 4 + 2 * d * tn * 2 + 6 * tm * tn * 4 + VMEM_SLACK),
        name="ffn_up",
    )(h, w_up, w_up)


def _mm_acc_kernel(x_ref, w_ref, xr_ref, wr_ref, o_ref):
    k = pl.program_id(2)
    last = pl.num_programs(2) - 1

    @pl.when(k == 0)
    def _():
        o_ref[...] = _dot(x_ref[...], w_ref[...].astype(BF16))

    @pl.when(jnp.logical_and(k > 0, k < last))
    def _():
        o_ref[...] += _dot(x_ref[...], w_ref[...].astype(BF16))

    @pl.when(k == last)
    def _():
        o_ref[...] += _dot(xr_ref[...], wr_ref[...].astype(BF16))


FFN_DOWN_TK = 768
FFN_DOWN_REM = 256


def _ffn_down(a, w_down, l, s):
    m, kdim = a.shape
    n = w_down.shape[-1]
    tm, tn, tk, tr = _mm_tiles(m), 1024, FFN_DOWN_TK, FFN_DOWN_REM
    nfull = kdim // tk
    assert kdim - nfull * tk == tr and (nfull * tk) % tr == 0
    rblk = nfull * tk // tr
    return pl.pallas_call(
        _mm_acc_kernel,
        grid=(m // tm, n // tn, nfull + 1),
        in_specs=[pl.BlockSpec((tm, tk), lambda i, j, k: (i, jnp.minimum(k, nfull - 1))),
                  pl.BlockSpec((None, None, tk, tn), lambda i, j, k: (l, s, jnp.minimum(k, nfull - 1), j)),
                  pl.BlockSpec((tm, tr), lambda i, j, k: (i, rblk)),
                  pl.BlockSpec((None, None, tr, tn), lambda i, j, k: (l, s, rblk, j))],
        out_specs=pl.BlockSpec((tm, tn), lambda i, j, k: (i, j)),
        out_shape=jax.ShapeDtypeStruct((m, n), F32),
        compiler_params=_params(("parallel", "parallel", "arbitrary"),
                                3 * tm * tn * 4 + 2 * tm * (tk + tr) * 2 + 2 * (tk + tr) * tn * 4 + tk * tn * 2
                                + VMEM_SLACK),
        name="ffn_down",
    )(a, w_down, a, w_down)


def _mm_plain_kernel(x_ref, w_ref, o_ref):
    o_ref[...] = _dot(x_ref[...], w_ref[...].astype(BF16)).astype(o_ref.dtype)


def _mm_plain(x, w, l, n, tn, name):
    m, d = x.shape
    tm = _mm_tiles(m)
    return pl.pallas_call(
        _mm_plain_kernel,
        grid=(m // tm, n // tn),
        in_specs=[pl.BlockSpec((tm, d), lambda i, j: (i, 0), pipeline_mode=pl.Buffered(1)),
                  pl.BlockSpec((None, d, tn), lambda i, j: (l, 0, j))],
        out_specs=pl.BlockSpec((tm, tn), lambda i, j: (i, j)),
        out_shape=jax.ShapeDtypeStruct((m, n), F32),
        compiler_params=_params(("parallel", "arbitrary"),
                                tm * d * 2 + 2 * d * tn * 4 + d * tn * 2 + 4 * tm * tn * 4 + VMEM_SLACK),
        name=name,
    )(x, w)


def _mm_gate_kernel(x_ref, w_ref, b_ref, o_ref):
    o_ref[...] = _sigmoid(_dot(x_ref[...], w_ref[...].astype(BF16)) + b_ref[...]).astype(o_ref.dtype)


def _gates(h, w_gate, b_gate3, l):
    m, d = h.shape
    n = w_gate.shape[-1]
    tm, tn = _mm_tiles(m), 256
    return pl.pallas_call(
        _mm_gate_kernel,
        grid=(m // tm, n // tn),
        in_specs=[pl.BlockSpec((tm, d), lambda i, j: (i, 0), pipeline_mode=pl.Buffered(1)),
                  pl.BlockSpec((None, d, tn), lambda i, j: (l, 0, j)),
                  pl.BlockSpec((None, 1, tn), lambda i, j: (l, 0, j))],
        out_specs=pl.BlockSpec((tm, tn), lambda i, j: (i, j)),
        out_shape=jax.ShapeDtypeStruct((m, n), BF16),
        compiler_params=_params(("parallel", "arbitrary"),
                                tm * d * 2 + 2 * d * tn * 4 + d * tn * 2 + 4 * tm * tn * 4 + VMEM_SLACK),
        name="gates",
    )(h, w_gate, b_gate3)


def _merge_kernel(ba_ref, bb_ref, bc_ref, bd_ref, w_ref, g0_ref, g1_ref, g2_ref, g3_ref, o_ref):
    r = 0
    out = None
    for x_ref, g_ref in ((ba_ref, g0_ref), (bb_ref, g1_ref), (bc_ref, g2_ref), (bd_ref, g3_ref)):
        kw = x_ref.shape[1]
        term = g_ref[...].astype(F32) * _dot(x_ref[...], w_ref[r:r + kw, :].astype(BF16))
        out = term if out is None else out + term
        r += kw
    o_ref[...] = out.astype(o_ref.dtype)


def _merge(br_a, br_b, br_c, br_d, w_branch, gates, l):
    m = br_a.shape[0]
    kall = w_branch.shape[1]
    tm, tn = _pick(m, (1040, 640, 320, 128)), 256
    ncol = D_MODEL // tn

    def br_spec(a):
        return pl.BlockSpec((tm, a.shape[1]), lambda i, j: (i, 0), pipeline_mode=pl.Buffered(1))

    def gate_spec(nb):
        return pl.BlockSpec((tm, tn), lambda i, j: (i, nb * ncol + j))

    return pl.pallas_call(
        _merge_kernel,
        grid=(m // tm, ncol),
        in_specs=[br_spec(br_a), br_spec(br_b), br_spec(br_c), br_spec(br_d),
                  pl.BlockSpec((None, kall, tn), lambda i, j: (l, 0, j)),
                  gate_spec(0), gate_spec(1), gate_spec(2), gate_spec(3)],
        out_specs=pl.BlockSpec((tm, tn), lambda i, j: (i, j)),
        out_shape=jax.ShapeDtypeStruct((m, D_MODEL), BF16),
        compiler_params=_params(("parallel", "arbitrary"),
                                tm * kall * 2 + 2 * kall * tn * 4 + kall * tn * 2 + 10 * tm * tn * 2
                                + 8 * tm * tn * 4 + VMEM_SLACK),
        name="merge",
    )(br_a, br_b, br_c, br_d, w_branch, gates, gates, gates, gates)


def _log_sigmoid(x):
    return jnp.minimum(x, 0.0) - jnp.log1p(jnp.exp(-jnp.abs(x)))


def _gla_gate_kernel(h_ref, wlr_ref, wup_ref, b_ref, o_ref):
    col = lax.broadcasted_iota(jnp.int32, wlr_ref.shape, 1)
    wlr = jnp.where(col < GLA_RANK, wlr_ref[...], 0.0)
    glr = _dot(h_ref[...], wlr.astype(BF16))
    pre = _dot(glr.astype(BF16), wup_ref[...].astype(BF16)) + b_ref[...]
    o_ref[...] = _log_sigmoid(pre) / GLA_TAU


def _gla_gate(h, w_in, w_up, b_up, l):
    m, d = h.shape
    tm = _pick(m, (640, 320, 128))
    return pl.pallas_call(
        _gla_gate_kernel,
        grid=(m // tm,),
        in_specs=[pl.BlockSpec((tm, d), lambda i: (i, 0)),
                  pl.BlockSpec((None, d, LANES), lambda i: (l, 0, OFF_GLR // LANES)),
                  pl.BlockSpec((None, LANES, GLA_QK), lambda i: (l, 0, 0)),
                  pl.BlockSpec((None, 1, GLA_QK), lambda i: (l, 0, 0))],
        out_specs=pl.BlockSpec((tm, GLA_QK), lambda i: (i, 0)),
        out_shape=jax.ShapeDtypeStruct((m, GLA_QK), F32),
        compiler_params=_params(("parallel",), 2 * tm * d * 2 + 3 * d * LANES * 4 + 6 * tm * GLA_QK * 4 + VMEM_SLACK),
        name="gla_gate",
    )(h, w_in, w_up, b_up)


SCAN_BLOCK = 256


def _mid_rows(G, m):
    L, K = G.shape
    if 2 * m >= SUBLANES:
        G3 = G.reshape(L // (2 * m), 2 * m, K)
        return jnp.broadcast_to(G3[:, m - 1:m, :], G3.shape).reshape(L, K)
    assert m == 2
    G3 = G.reshape(L // SUBLANES, SUBLANES, K)
    sub = lax.broadcasted_iota(jnp.int32, G3.shape, 1)
    lo = jnp.broadcast_to(G3[:, 1:2, :], G3.shape)
    hi = jnp.broadcast_to(G3[:, 5:6, :], G3.shape)
    return jnp.where(sub < 4, lo, hi).reshape(L, K)


def _cumsum_rows(g):
    L = g.shape[0]
    tri = (lax.broadcasted_iota(jnp.int32, (L, L), 0) >= lax.broadcasted_iota(jnp.int32, (L, L), 1)).astype(BF16)
    hi = g.astype(BF16)
    r1 = g - hi.astype(F32)
    mid = r1.astype(BF16)
    lo = (r1 - mid.astype(F32)).astype(BF16)
    return _dot(tri, hi) + _dot(tri, mid) + _dot(tri, lo)


def _gated_block(q, k, g, v, st):
    L, K = q.shape
    G = _cumsum_rows(g)
    row = lax.broadcasted_iota(jnp.int32, (L, 1), 0)
    x = lax.broadcasted_iota(jnp.int32, (L, L), 0) ^ lax.broadcasted_iota(jnp.int32, (L, L), 1)
    scores = jnp.where(x == 0, _dot_nt(q.astype(BF16), k.astype(BF16)), 0.0)
    m = L // 2
    while m >= 1:
        second = (row & m) != 0
        if m == 1:
            e = jnp.where(second, g, 0.0)
        else:
            gm = _mid_rows(G, m)
            e = jnp.where(second, G - gm, gm - G)
        sc = jnp.exp(e)
        qs = jnp.where(second, q * sc, 0.0).astype(BF16)
        ks = jnp.where(second, 0.0, k * sc).astype(BF16)
        s = _dot_nt(qs, ks)
        scores = scores + (s if 2 * m == L else jnp.where(x < 2 * m, s, 0.0))
        m //= 2
    o = _dot(scores.astype(BF16), v.astype(BF16))
    o = o + _dot_nt((q * jnp.exp(G)).astype(BF16), st.astype(BF16))
    g_last = G[L - 1:L, :]
    k_out = (k * jnp.exp(g_last - G)).astype(BF16)
    st_new = st * jnp.exp(g_last) + _dot(v.T.astype(BF16), k_out)
    return o, st_new


def _head_rms(o, g):
    return o * lax.rsqrt(jnp.mean(o * o, axis=-1, keepdims=True) + EPS) * g


def _hgrn_lb(logits, l):
    e = jnp.exp(logits - jnp.max(logits, axis=0, keepdims=True))
    p = e / jnp.sum(e, axis=0, keepdims=True)
    lb = jnp.zeros_like(p[0:1])
    for i in range(1, l + 1):
        lb = lb + p[i:i + 1]
    return lb


def _hgrn_qkg(hq, hf, lb):
    sig = _sigmoid(hf)
    q = _silu(hq)
    g = jnp.log(lb + (1.0 - lb) * sig)
    k = (1.0 - lb) * _sigmoid(-hf)
    return q, k, g


def _hgrn_prompt_kernel(hq_ref, hf_ref, hi_ref, hg_ref, lbl_ref, gn_ref, br_ref, s_ref, st_ref, *, l):
    t = pl.program_id(2)

    @pl.when(t == 0)
    def _():
        st_ref[...] = jnp.zeros_like(st_ref)

    q, k, g = _hgrn_qkg(hq_ref[...], hf_ref[...], _hgrn_lb(lbl_ref[...], l))
    o, st_new = _gated_block(q, k, g, hi_ref[...], st_ref[...])
    st_ref[...] = st_new
    br_ref[...] = (_head_rms(o, gn_ref[...]) * _silu(hg_ref[...])).astype(br_ref.dtype)

    @pl.when(t == pl.num_programs(2) - 1)
    def _():
        s_ref[...] = st_new.T


def _gla_prompt_kernel(gq_ref, gk_ref, gv_ref, gg_ref, la_ref, gn_ref, br_ref, s_ref, st_ref):
    t = pl.program_id(2)

    @pl.when(t == 0)
    def _():
        st_ref[...] = jnp.zeros_like(st_ref)

    o, st_new = _gated_block(gq_ref[...] * GLA_DK ** -0.5, gk_ref[...], la_ref[...], gv_ref[...], st_ref[...])
    st_ref[...] = st_new
    br_ref[...] = (_head_rms(o, gn_ref[...]) * _silu(gg_ref[...])).astype(br_ref.dtype)

    @pl.when(t == pl.num_programs(2) - 1)
    def _():
        s_ref[...] = st_new.T


def _scan_vmem(L, K, V):
    return (24 * L * L + 40 * L * K + 16 * L * V + 8 * K * V) * 4 + (8 << 20)


def _hgrn_prompt(z, lb_logits, norm_g3, l, bp, t_len):
    L = min(SCAN_BLOCK, t_len)
    nt = t_len // L
    K = V = HG_DK

    def col(off, w):
        return lambda b, h, t: (b * nt + t, off // w + h)

    return pl.pallas_call(
        functools.partial(_hgrn_prompt_kernel, l=l),
        grid=(bp, HG_HEADS, nt),
        in_specs=[pl.BlockSpec((L, K), col(OFF_HQ, K)), pl.BlockSpec((L, K), col(OFF_HF, K)),
                  pl.BlockSpec((L, V), col(OFF_HI, V)), pl.BlockSpec((L, V), col(OFF_HG, V)),
                  pl.BlockSpec((DEPTH, K), lambda b, h, t: (0, h)),
                  pl.BlockSpec((None, 1, V), lambda b, h, t: (l, 0, h))],
        out_specs=[pl.BlockSpec((L, V), lambda b, h, t: (b * nt + t, h)),
                   pl.BlockSpec((None, None, K, V), lambda b, h, t: (b, h, 0, 0))],
        out_shape=[jax.ShapeDtypeStruct((bp * t_len, HG_HEADS * V), BF16),
                   jax.ShapeDtypeStruct((bp, HG_HEADS, K, V), F32)],
        scratch_shapes=[pltpu.VMEM((V, K), F32)],
        compiler_params=_params(("parallel", "parallel", "arbitrary"), _scan_vmem(L, K, V)),
        name="hgrn_prompt",
    )(z, z, z, z, lb_logits, norm_g3)


def _gla_prompt(z, log_alpha, norm_g3, l, bp, t_len):
    L = min(SCAN_BLOCK, t_len)
    nt = t_len // L
    K, V = GLA_DK, GLA_DV

    def col(off, w):
        return lambda b, h, t: (b * nt + t, off // w + h)

    return pl.pallas_call(
        _gla_prompt_kernel,
        grid=(bp, GLA_HEADS, nt),
        in_specs=[pl.BlockSpec((L, K), col(OFF_GQ, K)), pl.BlockSpec((L, K), col(OFF_GK, K)),
                  pl.BlockSpec((L, V), col(OFF_GV, V)), pl.BlockSpec((L, V), col(OFF_GG, V)),
                  pl.BlockSpec((L, K), col(0, K)),
                  pl.BlockSpec((None, 1, V), lambda b, h, t: (l, 0, h))],
        out_specs=[pl.BlockSpec((L, V), lambda b, h, t: (b * nt + t, h)),
                   pl.BlockSpec((None, None, K, V), lambda b, h, t: (b, h, 0, 0))],
        out_shape=[jax.ShapeDtypeStruct((bp * t_len, GLA_HEADS * V), BF16),
                   jax.ShapeDtypeStruct((bp, GLA_HEADS, K, V), F32)],
        scratch_shapes=[pltpu.VMEM((V, K), F32)],
        compiler_params=_params(("parallel", "parallel", "arbitrary"), _scan_vmem(L, K, V)),
        name="gla_prompt",
    )(z, z, z, z, log_alpha, norm_g3)


def _rotate(x, cos, sin):
    half = x.shape[-1] // 2
    x1, x2 = x[:, :half], x[:, half:]
    return jnp.concatenate([x1 * cos - x2 * sin, x1 * sin + x2 * cos], axis=-1)


def _group_norm(o, g):
    c = o - jnp.mean(o, axis=-1, keepdims=True)
    return c * lax.rsqrt(jnp.mean(c * c, axis=-1, keepdims=True) + EPS) * g


def _ret_prompt_kernel(lg_ref, rq_ref, rk_ref, rv_ref, rg_ref, cos_ref, sin_ref, gn_ref, br_ref, s_ref, st_ref):
    h = pl.program_id(1)
    t = pl.program_id(2)
    lg = lg_ref[h]

    @pl.when(t == 0)
    def _():
        st_ref[...] = jnp.zeros_like(st_ref)

    cos, sin = cos_ref[...], sin_ref[...]
    q = _rotate(rq_ref[...], cos, sin)
    k = _rotate(rk_ref[...], cos, sin) * RET_DK ** -0.5
    v = rv_ref[...].astype(BF16)
    L, K = q.shape
    dist = (lax.broadcasted_iota(jnp.int32, (L, L), 0) - lax.broadcasted_iota(jnp.int32, (L, L), 1)).astype(F32)
    causal = dist >= 0
    decay = jnp.where(causal, jnp.exp(jnp.where(causal, dist * lg, 0.0)), 0.0)
    scores = _dot_nt(q.astype(BF16), k.astype(BF16)) * decay
    o = _dot(scores.astype(BF16), v)
    rowf = lax.broadcasted_iota(jnp.int32, (L, K), 0).astype(F32)
    st = st_ref[...]
    o = o + _dot((q * jnp.exp((rowf + 1.0) * lg)).astype(BF16), st.astype(BF16))
    k_out = k * jnp.exp((L - 1.0 - rowf) * lg)
    st_new = st * jnp.exp(jnp.full((1, 1), L, F32) * lg) + _dot(k_out.T.astype(BF16), v)
    st_ref[...] = st_new
    br_ref[...] = (_group_norm(o, gn_ref[...]) * _silu(rg_ref[...])).astype(br_ref.dtype)

    @pl.when(t == pl.num_programs(2) - 1)
    def _():
        s_ref[...] = st_new


def _ret_prompt(z, log_gamma, cos, sin, norm_g3, l, bp, t_len):
    L = min(SCAN_BLOCK, t_len)
    nt = t_len // L
    K, V = RET_DK, RET_DV

    def col(off, w):
        return lambda b, h, t: (b * nt + t, off // w + h)

    return pl.pallas_call(
        _ret_prompt_kernel,
        grid=(bp, RET_HEADS, nt),
        in_specs=[pl.BlockSpec(memory_space=pltpu.SMEM),
                  pl.BlockSpec((L, K), col(OFF_RQ, K)), pl.BlockSpec((L, K), col(OFF_RK, K)),
                  pl.BlockSpec((L, V), col(OFF_RV, V)), pl.BlockSpec((L, V), col(OFF_RG, V)),
                  pl.BlockSpec((L, K // 2), lambda b, h, t: (t, 0)),
                  pl.BlockSpec((L, K // 2), lambda b, h, t: (t, 0)),
                  pl.BlockSpec((None, 1, V), lambda b, h, t: (l, 0, h))],
        out_specs=[pl.BlockSpec((L, V), lambda b, h, t: (b * nt + t, h)),
                   pl.BlockSpec((None, None, K, V), lambda b, h, t: (b, h, 0, 0))],
        out_shape=[jax.ShapeDtypeStruct((bp * t_len, RET_HEADS * V), BF16),
                   jax.ShapeDtypeStruct((bp, RET_HEADS, K, V), F32)],
        scratch_shapes=[pltpu.VMEM((K, V), F32)],
        compiler_params=_params(("parallel", "parallel", "arbitrary"), _scan_vmem(L, K, V)),
        name="ret_prompt",
    )(log_gamma, z, z, z, z, cos, sin, norm_g3)


LRU_COLS = 256
CARRY_ROWS = SUBLANES


def _softplus(x):
    return jnp.maximum(x, 0.0) + jnp.log1p(jnp.exp(-jnp.abs(x)))


def _lru_gates(u, wa_ref, ba, wx_ref, bx, lam):
    nb = u.shape[1] // LRU_BW
    ra, rx = [], []
    for n in range(nb):
        un = u[:, n * LRU_BW:(n + 1) * LRU_BW].astype(BF16)
        ra.append(_dot(un, wa_ref[n].astype(BF16)))
        rx.append(_dot(un, wx_ref[n].astype(BF16)))
    r = _sigmoid(jnp.concatenate(ra, axis=1) + ba)
    i_g = _sigmoid(jnp.concatenate(rx, axis=1) + bx)
    log_a = -LRU_C * r * _softplus(-lam)
    a = jnp.exp(log_a)
    th = jnp.tanh(log_a)
    mult = jnp.sqrt(jnp.maximum(-2.0 * th / (1.0 - th), 0.0))
    return a, mult, i_g


def _lru_prompt_kernel(lx_ref, ly_ref, cw_ref, cb_ref, wa_ref, ba_ref, wx_ref, bx_ref, lam_ref,
                       br_ref, h_ref, c_ref, xbuf_ref, hprev_ref):
    t = pl.program_id(2)
    R = lx_ref.shape[0]

    @pl.when(t == 0)
    def _():
        xbuf_ref[0:CARRY_ROWS, :] = jnp.zeros((CARRY_ROWS, xbuf_ref.shape[1]), F32)
        hprev_ref[...] = jnp.zeros_like(hprev_ref)

    x = lx_ref[...]
    xbuf_ref[CARRY_ROWS:CARRY_ROWS + R, :] = x
    cw = cw_ref[...]
    u = cb_ref[...] + xbuf_ref[CARRY_ROWS - 3:CARRY_ROWS - 3 + R, :] * cw[0:1]
    u = u + xbuf_ref[CARRY_ROWS - 2:CARRY_ROWS - 2 + R, :] * cw[1:2]
    u = u + xbuf_ref[CARRY_ROWS - 1:CARRY_ROWS - 1 + R, :] * cw[2:3]
    u = u + x * cw[3:4]
    a, mult, i_g = _lru_gates(u, wa_ref, ba_ref[...], wx_ref, bx_ref[...], lam_ref[...])
    row = lax.broadcasted_iota(jnp.int32, (R, 1), 0)
    mult = jnp.where(row == jnp.where(t == 0, 0, -1), 1.0, mult)
    b = mult * i_g * u
    d = 1
    while d < R:
        valid = row >= d
        a_sh = pltpu.roll(a, d, 0)
        b_sh = pltpu.roll(b, d, 0)
        b = jnp.where(valid, a * b_sh + b, b)
        a = jnp.where(valid, a * a_sh, a)
        d *= 2
    hs = a * hprev_ref[0:1, :] + b
    br_ref[...] = (hs * jax.nn.gelu(ly_ref[...], approximate=True)).astype(br_ref.dtype)
    xbuf_ref[0:CARRY_ROWS, :] = x[R - CARRY_ROWS:R, :]
    hprev_ref[...] = jnp.broadcast_to(hs[R - 1:R, :], hprev_ref.shape)

    @pl.when(t == pl.num_programs(2) - 1)
    def _():
        h_ref[...] = hs[R - 1:R, :]
        c_ref[...] = x[R - (CONV_W - 1):R, :]


def _lru_prompt(z, P, l, bp, t_len):
    R = min(SCAN_BLOCK, t_len)
    nt = t_len // R
    C = LRU_COLS
    nc = LRU_WIDTH // C
    nb = C // LRU_BW
    vec = lambda: pl.BlockSpec((None, 1, C), lambda b, c, t: (l, 0, c))
    wblk = lambda: pl.BlockSpec((None, nb, LRU_BW, LRU_BW), lambda b, c, t: (l, c, 0, 0))
    return pl.pallas_call(
        _lru_prompt_kernel,
        grid=(bp, nc, nt),
        in_specs=[pl.BlockSpec((R, C), lambda b, c, t: (b * nt + t, OFF_LX // C + c)),
                  pl.BlockSpec((R, C), lambda b, c, t: (b * nt + t, OFF_LY // C + c)),
                  pl.BlockSpec((None, CONV_W, C), lambda b, c, t: (l, 0, c)),
                  vec(), wblk(), vec(), wblk(), vec(), vec()],
        out_specs=[pl.BlockSpec((R, C), lambda b, c, t: (b * nt + t, c)),
                   pl.BlockSpec((None, 1, C), lambda b, c, t: (b, 0, c)),
                   pl.BlockSpec((None, CONV_W - 1, C), lambda b, c, t: (b, 0, c))],
        out_shape=[jax.ShapeDtypeStruct((bp * t_len, LRU_WIDTH), BF16),
                   jax.ShapeDtypeStruct((bp, 1, LRU_WIDTH), F32),
                   jax.ShapeDtypeStruct((bp, CONV_W - 1, LRU_WIDTH), F32)],
        scratch_shapes=[pltpu.VMEM((R + CARRY_ROWS, C), F32), pltpu.VMEM((CARRY_ROWS, C), F32)],
        compiler_params=_params(("parallel", "parallel", "arbitrary"), 40 * R * C * 4 + (8 << 20)),
        name="lru_prompt",
    )(z, z, P['rglru_conv_w'], P['conv_b3'], P['rglru_w_a'], P['b_a3'], P['rglru_w_x'], P['b_x3'], P['lambda3'])


def _columns(rows):
    bb, K = rows.shape
    pad = jnp.concatenate([rows, jnp.zeros((LANES - bb, K), F32)], axis=0) if bb < LANES else rows
    return jnp.concatenate([pad[:, c * LANES:(c + 1) * LANES].T for c in range(K // LANES)], axis=0)


def _state_step(s_ref, so_ref, q, k, v, f_cols):
    bb, K = q.shape
    V = v.shape[1]
    k_cols = _columns(k)
    outs = []
    for i in range(bb):
        s = s_ref[i]
        kc = jnp.broadcast_to(k_cols[:, i:i + 1], (K, V))
        if f_cols.shape[0] == 1:
            s_new = s * f_cols + kc * v[i:i + 1, :]
        else:
            s_new = s * jnp.broadcast_to(f_cols[:, i:i + 1], (K, V)) + kc * v[i:i + 1, :]
        so_ref[i] = s_new
        qi = jnp.broadcast_to(q[i:i + 1, :], (SUBLANES, K)).astype(BF16)
        outs.append(_dot(qi, s_new.astype(BF16))[0:1, :])
    return jnp.concatenate(outs, axis=0)


def _hgrn_step_kernel(hq_ref, hf_ref, hi_ref, hg_ref, lbl_ref, gn_ref, s_ref, br_ref, so_ref, *, l):
    lb = _hgrn_lb(lbl_ref[...], l)
    q, k, g = _hgrn_qkg(hq_ref[...], hf_ref[...], lb)
    o = _state_step(s_ref, so_ref, q, k, hi_ref[...], _columns(jnp.exp(g)))
    br_ref[...] = (_head_rms(o, gn_ref[...]) * _silu(hg_ref[...])).astype(br_ref.dtype)


def _gla_step_kernel(gq_ref, gk_ref, gv_ref, gg_ref, la_ref, gn_ref, s_ref, br_ref, so_ref):
    o = _state_step(s_ref, so_ref, gq_ref[...] * GLA_DK ** -0.5, gk_ref[...], gv_ref[...],
                    _columns(jnp.exp(la_ref[...])))
    br_ref[...] = (_head_rms(o, gn_ref[...]) * _silu(gg_ref[...])).astype(br_ref.dtype)


def _ret_step_kernel(lg_ref, rq_ref, rk_ref, rv_ref, rg_ref, cos_ref, sin_ref, gn_ref, s_ref, br_ref, so_ref):
    lg = lg_ref[pl.program_id(0)]
    cos, sin = cos_ref[...], sin_ref[...]
    q = _rotate(rq_ref[...], cos, sin)
    k = _rotate(rk_ref[...], cos, sin) * RET_DK ** -0.5
    gamma = jnp.exp(jnp.full((1, 1), 1.0, F32) * lg)
    o = _state_step(s_ref, so_ref, q, k, rv_ref[...], gamma)
    br_ref[...] = (_group_norm(o, gn_ref[...]) * _silu(rg_ref[...])).astype(br_ref.dtype)


def _step_call(kernel, z, extra, extra_specs, state, new_states, l, row0, offs, K, V, H, bb, name, smem=None):
    bs = state.shape[1]
    rb = row0 // bb

    def col(off, w):
        return pl.BlockSpec((bb, w), lambda h, b: (rb + b, off // w + h))

    in_specs = [col(offs[0], K), col(offs[1], K), col(offs[2], V), col(offs[3], V)] + extra_specs
    in_specs.append(pl.BlockSpec((None, bb, None, K, V), lambda h, b: (l, b, h, 0, 0)))
    args = [z, z, z, z] + extra + [state]
    if smem is not None:
        in_specs = [pl.BlockSpec(memory_space=pltpu.SMEM)] + in_specs
        args = [smem] + args
    in_specs.append(pl.BlockSpec(memory_space=pl.ANY))
    aliases = {len(args): 1}
    return pl.pallas_call(
        functools.partial(_skip_ref, kernel, len(args)),
        grid=(H, bs // bb),
        in_specs=in_specs,
        out_specs=[pl.BlockSpec((bb, V), lambda h, b: (b, h)),
                   pl.BlockSpec((None, bb, None, K, V), lambda h, b: (l, b, h, 0, 0))],
        out_shape=[jax.ShapeDtypeStruct((bs, H * V), BF16),
                   jax.ShapeDtypeStruct(state.shape, F32)],
        input_output_aliases=aliases,
        compiler_params=_params(("parallel", "parallel"), 6 * bb * K * V * 4 + (8 << 20)),
        name=name,
    )(*args, new_states)


def _skip_ref(kernel, idx, *refs):
    return kernel(*refs[:idx], *refs[idx + 1:])


def _lru_step_kernel(lx_ref, ly_ref, cw_ref, cb_ref, wa_ref, ba_ref, wx_ref, bx_ref, lam_ref, c_ref, h_ref,
                     br_ref, ho_ref, co_ref):
    x = lx_ref[...]
    cw = cw_ref[...]
    u = cb_ref[...] + c_ref[0] * cw[0:1]
    u = u + c_ref[1] * cw[1:2]
    u = u + c_ref[2] * cw[2:3]
    u = u + x * cw[3:4]
    a, mult, i_g = _lru_gates(u, wa_ref, ba_ref[...], wx_ref, bx_ref[...], lam_ref[...])
    hs = a * h_ref[...] + mult * i_g * u
    br_ref[...] = (hs * jax.nn.gelu(ly_ref[...], approximate=True)).astype(br_ref.dtype)
    ho_ref[...] = hs
    co_ref[0] = c_ref[1]
    co_ref[1] = c_ref[2]
    co_ref[2] = x


def _lru_step(z, P, conv_state_t, h_state, l, row0):
    bs = h_state.shape[1]
    W = LRU_WIDTH
    vec = lambda: pl.BlockSpec((None, 1, W), lambda i: (l, 0, 0))
    wblk = lambda: pl.BlockSpec((None, LRU_BLOCKS, LRU_BW, LRU_BW), lambda i: (l, 0, 0, 0))
    return pl.pallas_call(
        _lru_step_kernel,
        grid=(1,),
        in_specs=[pl.BlockSpec((bs, W), lambda i: (row0 // bs, OFF_LX // W)),
                  pl.BlockSpec((bs, W), lambda i: (row0 // bs, OFF_LY // W)),
                  pl.BlockSpec((None, CONV_W, W), lambda i: (l, 0, 0)),
                  vec(), wblk(), vec(), wblk(), vec(), vec(),
                  pl.BlockSpec((None, CONV_W - 1, bs, W), lambda i: (l, 0, 0, 0)),
                  pl.BlockSpec((None, bs, W), lambda i: (l, 0, 0))],
        out_specs=[pl.BlockSpec((bs, W), lambda i: (0, 0)),
                   pl.BlockSpec((bs, W), lambda i: (0, 0)),
                   pl.BlockSpec((CONV_W - 1, bs, W), lambda i: (0, 0, 0))],
        out_shape=[jax.ShapeDtypeStruct((bs, W), BF16), jax.ShapeDtypeStruct((bs, W), F32),
                   jax.ShapeDtypeStruct((CONV_W - 1, bs, W), F32)],
        compiler_params=_params(("arbitrary",), 64 * bs * W * 4 + (8 << 20)),
        name="lru_step",
    )(z, z, P['rglru_conv_w'], P['conv_b3'], P['rglru_w_a'], P['b_a3'], P['rglru_w_x'], P['b_x3'], P['lambda3'],
      conv_state_t, h_state)


def _rope_tables(pos):
    half = RET_DK // 2
    freqs = jnp.exp(-math.log(ROPE_BASE) * jnp.arange(half, dtype=F32) / half)
    ang = pos.astype(F32)[:, None] * freqs[None]
    return jnp.cos(ang), jnp.sin(ang)


def _mixer(h, st, stacked, l, P, bp, t_len, bs):
    ns_hg, ns_rt, ns_gl = stacked
    mp = bp * t_len
    z = _mm_plain(h, P['w_in'], l, D_Z, 256, "w_in")
    log_alpha = _gla_gate(h, P['w_in'], P['w_up_pad'], P['b_up3'], l)
    gates = _gates(h, P['w_gate'], P['b_gate3'], l)

    a_p, s_hg_p = _hgrn_prompt(z, P['hgrn_lb_logits'], P['hgrn_g3'], l, bp, t_len)
    b_p, s_lh_p, s_lc_p = _lru_prompt(z, P, l, bp, t_len)
    c_p, s_rt_p = _ret_prompt(z, P['log_gamma'], P['cos_p'], P['sin_p'], P['ret_g3'], l, bp, t_len)
    d_p, s_gl_p = _gla_prompt(z, log_alpha, P['gla_g3'], l, bp, t_len)

    s_hg, s_lh, s_lc, s_rt, s_gl = st
    bb = 16
    a_s, s_hg_s = _step_call(
        functools.partial(_hgrn_step_kernel, l=l), z,
        [P['hgrn_lb_logits'], P['hgrn_g3']],
        [pl.BlockSpec((DEPTH, HG_DK), lambda h_, b: (0, h_)),
         pl.BlockSpec((None, 1, HG_DK), lambda h_, b: (l, 0, h_))],
        s_hg, ns_hg, l, mp, (OFF_HQ, OFF_HF, OFF_HI, OFF_HG), HG_DK, HG_DK, HG_HEADS, bb, "hgrn_step")
    b_s, s_lh_s, s_lc_s = _lru_step(z, P, s_lc, s_lh, l, mp)
    s_lc_s = jnp.swapaxes(s_lc_s, 0, 1)
    c_s, s_rt_s = _step_call(
        _ret_step_kernel, z,
        [P['cos_s'], P['sin_s'], P['ret_g3']],
        [pl.BlockSpec((1, RET_DK // 2), lambda h_, b: (0, 0)),
         pl.BlockSpec((1, RET_DK // 2), lambda h_, b: (0, 0)),
         pl.BlockSpec((None, 1, RET_DV), lambda h_, b: (l, 0, h_))],
        s_rt, ns_rt, l, mp, (OFF_RQ, OFF_RK, OFF_RV, OFF_RG), RET_DK, RET_DV, RET_HEADS, 8, "ret_step",
        smem=P['log_gamma'])
    d_s, s_gl_s = _step_call(
        _gla_step_kernel, z,
        [log_alpha, P['gla_g3']],
        [pl.BlockSpec((bb, GLA_DK), lambda h_, b: (mp // bb + b, h_)),
         pl.BlockSpec((None, 1, GLA_DV), lambda h_, b: (l, 0, h_))],
        s_gl, ns_gl, l, mp, (OFF_GQ, OFF_GK, OFF_GV, OFF_GG), GLA_DK, GLA_DV, GLA_HEADS, bb, "gla_step")

    cat = lambda p, s: jnp.concatenate([p, s], axis=0)
    merged = _merge(cat(a_p, a_s), cat(b_p, b_s), cat(c_p, c_s), cat(d_p, d_s), P['w_branch'], gates, l)
    out = _mm_plain(merged, P['w_out'], l, D_MODEL, 256, "w_out")
    new_p = (s_hg_p, s_lh_p.reshape(bp, LRU_WIDTH), s_lc_p, s_rt_p, s_gl_p)
    new_s = (s_hg_s, s_lh_s, s_lc_s, s_rt_s, s_gl_s)
    return out, new_p, new_s


def _layer(x, h, st, stacked, l, P, bp, t_len, bs, last):
    g = P['norm_g']
    f1 = _ffn_down(_ffn_up(h, P['w_ff_up'], l, 0), P['w_ff_down'], l, 0)
    x, h = _resid_norm(x, f1, g[l, 1:2], 0.5, g[l, 2:3])
    m, new_p, new_s = _mixer(h, st, stacked, l, P, bp, t_len, bs)
    x, h = _resid_norm(x, m, g[l, 3:4], 1.0, g[l, 4:5])
    f2 = _ffn_down(_ffn_up(h, P['w_ff_up'], l, 1), P['w_ff_down'], l, 1)
    x, h = _resid_norm(x, f2, g[l, 5:6], 0.5, None if last else g[l + 1, 0:1])
    return x, h, new_p, new_s


def kernel(x_prompt, x_sample, state_hgrn, state_rglru_h, state_rglru_conv, state_retention, state_gla,
           norm_g, w_ff_up, w_ff_down, w_in, hgrn_lb_logits, hgrn_norm_g,
           rglru_conv_w, rglru_conv_b, rglru_w_a, rglru_b_a, rglru_w_x, rglru_b_x, rglru_lambda,
           ret_norm_g, gla_w_up, gla_b_up, gla_norm_g, w_branch, w_gate, b_gate, w_out):
    bp, t_len, d = x_prompt.shape
    bs = x_sample.shape[0]
    depth = norm_g.shape[0]
    P = _prep(t_len, norm_g, w_ff_up, w_ff_down, w_in, hgrn_lb_logits, hgrn_norm_g,
              rglru_conv_w, rglru_conv_b, rglru_w_a, rglru_b_a, rglru_w_x, rglru_b_x, rglru_lambda,
              ret_norm_g, gla_w_up, gla_b_up, gla_norm_g, w_branch, w_gate, b_gate, w_out)
    st = (state_hgrn, state_rglru_h, jnp.swapaxes(state_rglru_conv, 1, 2), state_retention, state_gla)

    x = jnp.concatenate([x_prompt.reshape(bp * t_len, d), x_sample.reshape(bs, d)], axis=0)
    h = _rmsnorm_rows(x, norm_g[0, 0:1])
    new_p, new_s = [], []
    stacked = tuple(jnp.zeros_like(s) for s in (state_hgrn, state_retention, state_gla))
    for l in range(depth):
        x, h, stp, sts = _layer(x, h, st, stacked, l, P, bp, t_len, bs, l == depth - 1)
        stacked = (sts[0], sts[3], sts[4])
        new_p.append(stp)
        new_s.append(sts)
    yp = x[:bp * t_len].reshape(bp, t_len, d)
    ys = x[bp * t_len:].reshape(bs, 1, d)
    sp = [jnp.stack([s[i] for s in new_p]) for i in range(5)]
    s_lh = jnp.stack([s[1] for s in new_s])
    s_lc = jnp.stack([s[2] for s in new_s])
    return (yp, ys, sp[0], sp[1], sp[2], sp[3], sp[4], stacked[0], s_lh, s_lc, stacked[1], stacked[2])


def _prep(t_len, norm_g, w_ff_up, w_ff_down, w_in, hgrn_lb_logits, hgrn_norm_g,
          rglru_conv_w, rglru_conv_b, rglru_w_a, rglru_b_a, rglru_w_x, rglru_b_x, rglru_lambda,
          ret_norm_g, gla_w_up, gla_b_up, gla_norm_g, w_branch, w_gate, b_gate, w_out):
    row3 = lambda a: a.reshape(a.shape[0], 1, a.shape[1])
    cos_p, sin_p = _rope_tables(jnp.arange(t_len, dtype=jnp.int32))
    cos_s, sin_s = _rope_tables(PAST_LEN + jnp.arange(1, dtype=jnp.int32))
    P = dict(norm_g=norm_g, w_ff_up=w_ff_up, w_ff_down=w_ff_down, w_in=w_in,
             hgrn_lb_logits=hgrn_lb_logits, hgrn_g3=row3(hgrn_norm_g),
             rglru_conv_w=rglru_conv_w, conv_b3=row3(rglru_conv_b), rglru_w_a=rglru_w_a, b_a3=row3(rglru_b_a),
             rglru_w_x=rglru_w_x, b_x3=row3(rglru_b_x), lambda3=row3(rglru_lambda),
             ret_g3=row3(ret_norm_g), gla_g3=row3(gla_norm_g), b_up3=row3(gla_b_up),
             w_up_pad=jnp.pad(gla_w_up, ((0, 0), (0, LANES - GLA_RANK), (0, 0))),
             w_branch=w_branch, w_gate=w_gate, b_gate3=row3(b_gate), w_out=w_out,
             log_gamma=jnp.log1p(-jnp.exp2(-5.0 - jnp.arange(RET_HEADS, dtype=F32))),
             cos_p=cos_p, sin_p=sin_p, cos_s=cos_s, sin_s=sin_s)
    return P
```

```python
import functools
import math

import jax
import jax.numpy as jnp
from jax import lax
from jax.experimental import pallas as pl
from jax.experimental.pallas import tpu as pltpu

F32 = jnp.float32
BF16 = jnp.bfloat16

D_MODEL = 4096
DEPTH = 4
PAST_LEN = 16384
N_BRANCH = 4
BRANCH_WIDTH = D_MODEL // 4
HG_HEADS = 8
HG_DK = BRANCH_WIDTH // HG_HEADS
LRU_WIDTH = BRANCH_WIDTH
LRU_BLOCKS = 8
LRU_BW = LRU_WIDTH // LRU_BLOCKS
CONV_W = 4
LRU_C = 8.0
RET_HEADS = 4
RET_DK = BRANCH_WIDTH // RET_HEADS
RET_DV = 2 * RET_DK
ROPE_BASE = 10000.0
GLA_HEADS = 4
GLA_DK = BRANCH_WIDTH // 2 // GLA_HEADS
GLA_DV = BRANCH_WIDTH // GLA_HEADS
GLA_QK = GLA_HEADS * GLA_DK
GLA_RANK = 16
GLA_TAU = 16.0
D_FF = 256 * ((8 * D_MODEL // 3 + 255) // 256)
EPS = 1e-6

OFF_HQ, OFF_HF, OFF_HI, OFF_HG = 0, 1024, 2048, 3072
OFF_LX, OFF_LY = 4096, 5120
OFF_RQ, OFF_RK, OFF_RV, OFF_RG = 6144, 7168, 8192, 10240
OFF_GQ, OFF_GK, OFF_GV, OFF_GG = 12288, 12800, 13312, 14336
OFF_GLR = 15360
D_Z = OFF_GLR

LANES = 128
SUBLANES = 8
VMEM_LIMIT_CAP = 58 * 1024 * 1024
VMEM_SLACK = 12 * 1024 * 1024


def _params(semantics, vmem_bytes):
    return pltpu.CompilerParams(dimension_semantics=semantics,
                                vmem_limit_bytes=int(min(max(vmem_bytes, 16 << 20), VMEM_LIMIT_CAP)))


def _pick(n, candidates):
    for c in candidates:
        if n % c == 0:
            return c
    raise ValueError(f"no tile of {candidates} divides {n}")


def _sigmoid(x):
    return 1.0 / (1.0 + jnp.exp(-x))


def _silu(x):
    return x * _sigmoid(x)


def _dot(a, b):
    return jnp.dot(a, b, preferred_element_type=F32)


def _dot_nt(a, b):
    return lax.dot_general(a, b, (((1,), (1,)), ((), ())), preferred_element_type=F32)


def _rms(x, g):
    return x * lax.rsqrt(jnp.mean(x * x, axis=-1, keepdims=True) + EPS) * g


def _norm_kernel(x_ref, g_ref, h_ref):
    h_ref[...] = _rms(x_ref[...], g_ref[...]).astype(h_ref.dtype)


def _rmsnorm_rows(x, g_row):
    m, d = x.shape
    tm = _pick(m, (320, 160, 128, 64, 32, 16))
    return pl.pallas_call(
        _norm_kernel,
        grid=(m // tm,),
        in_specs=[pl.BlockSpec((tm, d), lambda i: (i, 0)), pl.BlockSpec((1, d), lambda i: (0, 0))],
        out_specs=pl.BlockSpec((tm, d), lambda i: (i, 0)),
        out_shape=jax.ShapeDtypeStruct((m, d), BF16),
        compiler_params=_params(("parallel",), tm * d * 24 + VMEM_SLACK),
        name="rmsnorm",
    )(x, g_row)


def _resid_norm_kernel(x_ref, f_ref, gp_ref, gn_ref, xo_ref, h_ref, *, scale):
    xn = x_ref[...] + scale * _rms(f_ref[...], gp_ref[...])
    xo_ref[...] = xn
    h_ref[...] = _rms(xn, gn_ref[...]).astype(h_ref.dtype)


def _resid_kernel(x_ref, f_ref, gp_ref, xo_ref, *, scale):
    xo_ref[...] = x_ref[...] + scale * _rms(f_ref[...], gp_ref[...])


def _resid_norm(x, f, g_post, scale, g_next):
    m, d = x.shape
    tm = _pick(m, (160, 128, 64, 32, 16))
    row = pl.BlockSpec((tm, d), lambda i: (i, 0))
    vec = pl.BlockSpec((1, d), lambda i: (0, 0))
    if g_next is None:
        return pl.pallas_call(
            functools.partial(_resid_kernel, scale=scale),
            grid=(m // tm,), in_specs=[row, row, vec], out_specs=row,
            out_shape=jax.ShapeDtypeStruct((m, d), F32),
            compiler_params=_params(("parallel",), tm * d * 40 + VMEM_SLACK),
            name="resid",
        )(x, f, g_post), None
    return pl.pallas_call(
        functools.partial(_resid_norm_kernel, scale=scale),
        grid=(m // tm,), in_specs=[row, row, vec, vec], out_specs=[row, row],
        out_shape=[jax.ShapeDtypeStruct((m, d), F32), jax.ShapeDtypeStruct((m, d), BF16)],
        compiler_params=_params(("parallel",), tm * d * 48 + VMEM_SLACK),
        name="resid_norm",
    )(x, f, g_post, g_next)


def _mm_tiles(m):
    return _pick(m, (2080, 1040, 640, 320, 128))


def _mm_swiglu_kernel(x_ref, wg_ref, wu_ref, o_ref):
    x = x_ref[...]
    g = _dot(x, wg_ref[...].astype(BF16))
    u = _dot(x, wu_ref[...].astype(BF16))
    o_ref[...] = (_silu(g) * u).astype(o_ref.dtype)


def _ffn_up(h, w_up, l, s):
    m, d = h.shape
    tm, tn = _mm_tiles(m), 256
    noff = D_FF // tn
    return pl.pallas_call(
        _mm_swiglu_kernel,
        grid=(m // tm, D_FF // tn),
        in_specs=[pl.BlockSpec((tm, d), lambda i, j: (i, 0), pipeline_mode=pl.Buffered(1)),
                  pl.BlockSpec((None, None, d, tn), lambda i, j: (l, s, 0, j)),
                  pl.BlockSpec((None, None, d, tn), lambda i, j: (l, s, 0, j + noff))],
        out_specs=pl.BlockSpec((tm, tn), lambda i, j: (i, j)),
        out_shape=jax.ShapeDtypeStruct((m, D_FF), BF16),
        compiler_params=_params(("parallel", "arbitrary"),
                                tm * d * 2 + 4 * d * tn * 4 + 2 * d * tn * 2 + 6 * tm * tn * 4 + VMEM_SLACK),
        name="ffn_up",
    )(h, w_up, w_up)


def _mm_acc_kernel(x_ref, w_ref, xr_ref, wr_ref, o_ref):
    k = pl.program_id(2)
    last = pl.num_programs(2) - 1

    @pl.when(k == 0)
    def _():
        o_ref[...] = _dot(x_ref[...], w_ref[...].astype(BF16))

    @pl.when(jnp.logical_and(k > 0, k < last))
    def _():
        o_ref[...] += _dot(x_ref[...], w_ref[...].astype(BF16))

    @pl.when(k == last)
    def _():
        o_ref[...] += _dot(xr_ref[...], wr_ref[...].astype(BF16))


FFN_DOWN_TK = 768
FFN_DOWN_REM = 256


def _ffn_down(a, w_down, l, s):
    m, kdim = a.shape
    n = w_down.shape[-1]
    tm, tn, tk, tr = _mm_tiles(m), 1024, FFN_DOWN_TK, FFN_DOWN_REM
    nfull = kdim // tk
    assert kdim - nfull * tk == tr and (nfull * tk) % tr == 0
    rblk = nfull * tk // tr
    return pl.pallas_call(
        _mm_acc_kernel,
        grid=(m // tm, n // tn, nfull + 1),
        in_specs=[pl.BlockSpec((tm, tk), lambda i, j, k: (i, jnp.minimum(k, nfull - 1))),
                  pl.BlockSpec((None, None, tk, tn), lambda i, j, k: (l, s, jnp.minimum(k, nfull - 1), j)),
                  pl.BlockSpec((tm, tr), lambda i, j, k: (i, rblk)),
                  pl.BlockSpec((None, None, tr, tn), lambda i, j, k: (l, s, rblk, j))],
        out_specs=pl.BlockSpec((tm, tn), lambda i, j, k: (i, j)),
        out_shape=jax.ShapeDtypeStruct((m, n), F32),
        compiler_params=_params(("parallel", "parallel", "arbitrary"),
                                3 * tm * tn * 4 + 2 * tm * (tk + tr) * 2 + 2 * (tk + tr) * tn * 4 + tk * tn * 2
                                + VMEM_SLACK),
        name="ffn_down",
    )(a, w_down, a, w_down)


def _mm_plain_kernel(x_ref, w_ref, o_ref):
    o_ref[...] = _dot(x_ref[...], w_ref[...].astype(BF16)).astype(o_ref.dtype)


def _mm_plain_nt_kernel(x_ref, wt_ref, o_ref):
    o_ref[...] = _dot_nt(x_ref[...], wt_ref[...].astype(BF16)).astype(o_ref.dtype)


def _mm_plain(x, w, l, n, tn, name, transposed=False):
    m, d = x.shape
    tm = _mm_tiles(m)
    w_spec = (pl.BlockSpec((None, tn, d), lambda i, j: (l, j, 0)) if transposed
              else pl.BlockSpec((None, d, tn), lambda i, j: (l, 0, j)))
    return pl.pallas_call(
        _mm_plain_nt_kernel if transposed else _mm_plain_kernel,
        grid=(m // tm, n // tn),
        in_specs=[pl.BlockSpec((tm, d), lambda i, j: (i, 0), pipeline_mode=pl.Buffered(1)), w_spec],
        out_specs=pl.BlockSpec((tm, tn), lambda i, j: (i, j)),
        out_shape=jax.ShapeDtypeStruct((m, n), F32),
        compiler_params=_params(("parallel", "arbitrary"),
                                tm * d * 2 + 2 * d * tn * 4 + d * tn * 2 + 4 * tm * tn * 4 + VMEM_SLACK),
        name=name,
    )(x, w)


def _mm_gate_kernel(x_ref, w_ref, b_ref, o_ref):
    o_ref[...] = _sigmoid(_dot(x_ref[...], w_ref[...].astype(BF16)) + b_ref[...]).astype(o_ref.dtype)


def _gates(h, w_gate, b_gate3, l):
    m, d = h.shape
    n = w_gate.shape[-1]
    tm, tn = _mm_tiles(m), 512
    return pl.pallas_call(
        _mm_gate_kernel,
        grid=(m // tm, n // tn),
        in_specs=[pl.BlockSpec((tm, d), lambda i, j: (i, 0), pipeline_mode=pl.Buffered(1)),
                  pl.BlockSpec((None, d, tn), lambda i, j: (l, 0, j)),
                  pl.BlockSpec((None, 1, tn), lambda i, j: (l, 0, j))],
        out_specs=pl.BlockSpec((tm, tn), lambda i, j: (i, j)),
        out_shape=jax.ShapeDtypeStruct((m, n), BF16),
        compiler_params=_params(("parallel", "arbitrary"),
                                tm * d * 2 + 2 * d * tn * 4 + d * tn * 2 + 4 * tm * tn * 4 + VMEM_SLACK),
        name="gates",
    )(h, w_gate, b_gate3)


def _merge_kernel(ba_ref, bb_ref, bc_ref, bd_ref, w_ref, g0_ref, g1_ref, g2_ref, g3_ref, o_ref):
    r = 0
    out = None
    for x_ref, g_ref in ((ba_ref, g0_ref), (bb_ref, g1_ref), (bc_ref, g2_ref), (bd_ref, g3_ref)):
        kw = x_ref.shape[1]
        term = g_ref[...].astype(F32) * _dot(x_ref[...], w_ref[r:r + kw, :].astype(BF16))
        out = term if out is None else out + term
        r += kw
    o_ref[...] = out.astype(o_ref.dtype)


def _merge(br_a, br_b, br_c, br_d, w_branch, gates, l):
    m = br_a.shape[0]
    kall = w_branch.shape[1]
    tm, tn = _pick(m, (1040, 640, 320, 128)), 256
    ncol = D_MODEL // tn

    def br_spec(a):
        return pl.BlockSpec((tm, a.shape[1]), lambda i, j: (i, 0), pipeline_mode=pl.Buffered(1))

    def gate_spec(nb):
        return pl.BlockSpec((tm, tn), lambda i, j: (i, nb * ncol + j))

    return pl.pallas_call(
        _merge_kernel,
        grid=(m // tm, ncol),
        in_specs=[br_spec(br_a), br_spec(br_b), br_spec(br_c), br_spec(br_d),
                  pl.BlockSpec((None, kall, tn), lambda i, j: (l, 0, j)),
                  gate_spec(0), gate_spec(1), gate_spec(2), gate_spec(3)],
        out_specs=pl.BlockSpec((tm, tn), lambda i, j: (i, j)),
        out_shape=jax.ShapeDtypeStruct((m, D_MODEL), BF16),
        compiler_params=_params(("parallel", "arbitrary"),
                                tm * kall * 2 + 2 * kall * tn * 4 + kall * tn * 2 + 10 * tm * tn * 2
                                + 8 * tm * tn * 4 + VMEM_SLACK),
        name="merge",
    )(br_a, br_b, br_c, br_d, w_branch, gates, gates, gates, gates)


def _log_sigmoid(x):
    return jnp.minimum(x, 0.0) - jnp.log1p(jnp.exp(-jnp.abs(x)))


def _gla_gate_kernel(h_ref, wlr_ref, wup_ref, b_ref, o_ref):
    row = lax.broadcasted_iota(jnp.int32, wlr_ref.shape, 0)
    wlr = jnp.where(row < GLA_RANK, wlr_ref[...], 0.0)
    glr = _dot_nt(h_ref[...], wlr.astype(BF16))
    pre = _dot(glr.astype(BF16), wup_ref[...].astype(BF16)) + b_ref[...]
    o_ref[...] = _log_sigmoid(pre) / GLA_TAU


def _gla_gate(h, w_in_t, w_up, b_up, l):
    m, d = h.shape
    tm = _pick(m, (640, 320, 128))
    return pl.pallas_call(
        _gla_gate_kernel,
        grid=(m // tm,),
        in_specs=[pl.BlockSpec((tm, d), lambda i: (i, 0)),
                  pl.BlockSpec((None, LANES, d), lambda i: (l, OFF_GLR // LANES, 0)),
                  pl.BlockSpec((None, LANES, GLA_QK), lambda i: (l, 0, 0)),
                  pl.BlockSpec((None, 1, GLA_QK), lambda i: (l, 0, 0))],
        out_specs=pl.BlockSpec((tm, GLA_QK), lambda i: (i, 0)),
        out_shape=jax.ShapeDtypeStruct((m, GLA_QK), F32),
        compiler_params=_params(("parallel",), 2 * tm * d * 2 + 3 * d * LANES * 4 + 6 * tm * GLA_QK * 4 + VMEM_SLACK),
        name="gla_gate",
    )(h, w_in_t, w_up, b_up)


SCAN_BLOCK = 256


def _mid_rows(G, m):
    L, K = G.shape
    if 2 * m >= SUBLANES:
        G3 = G.reshape(L // (2 * m), 2 * m, K)
        return jnp.broadcast_to(G3[:, m - 1:m, :], G3.shape).reshape(L, K)
    assert m == 2
    G3 = G.reshape(L // SUBLANES, SUBLANES, K)
    sub = lax.broadcasted_iota(jnp.int32, G3.shape, 1)
    lo = jnp.broadcast_to(G3[:, 1:2, :], G3.shape)
    hi = jnp.broadcast_to(G3[:, 5:6, :], G3.shape)
    return jnp.where(sub < 4, lo, hi).reshape(L, K)


def _cumsum_rows(g):
    L = g.shape[0]
    tri = (lax.broadcasted_iota(jnp.int32, (L, L), 0) >= lax.broadcasted_iota(jnp.int32, (L, L), 1)).astype(BF16)
    hi = g.astype(BF16)
    r1 = g - hi.astype(F32)
    mid = r1.astype(BF16)
    lo = (r1 - mid.astype(F32)).astype(BF16)
    return _dot(tri, hi) + _dot(tri, mid) + _dot(tri, lo)


def _gated_block(q, k, g, v, st):
    L, K = q.shape
    G = _cumsum_rows(g)
    row = lax.broadcasted_iota(jnp.int32, (L, 1), 0)
    x = lax.broadcasted_iota(jnp.int32, (L, L), 0) ^ lax.broadcasted_iota(jnp.int32, (L, L), 1)
    scores = jnp.where(x == 0, _dot_nt(q.astype(BF16), k.astype(BF16)), 0.0)
    m = L // 2
    while m >= 1:
        second = (row & m) != 0
        if m == 1:
            e = jnp.where(second, g, 0.0)
        else:
            gm = _mid_rows(G, m)
            e = jnp.where(second, G - gm, gm - G)
        sc = jnp.exp(e)
        qs = jnp.where(second, q * sc, 0.0).astype(BF16)
        ks = jnp.where(second, 0.0, k * sc).astype(BF16)
        s = _dot_nt(qs, ks)
        scores = scores + (s if 2 * m == L else jnp.where(x < 2 * m, s, 0.0))
        m //= 2
    o = _dot(scores.astype(BF16), v.astype(BF16))
    o = o + _dot_nt((q * jnp.exp(G)).astype(BF16), st.astype(BF16))
    g_last = G[L - 1:L, :]
    k_out = (k * jnp.exp(g_last - G)).astype(BF16)
    st_new = st * jnp.exp(g_last) + _dot(v.T.astype(BF16), k_out)
    return o, st_new


def _head_rms(o, g):
    return o * lax.rsqrt(jnp.mean(o * o, axis=-1, keepdims=True) + EPS) * g


def _hgrn_lb(logits, l):
    e = jnp.exp(logits - jnp.max(logits, axis=0, keepdims=True))
    p = e / jnp.sum(e, axis=0, keepdims=True)
    lb = jnp.zeros_like(p[0:1])
    for i in range(1, l + 1):
        lb = lb + p[i:i + 1]
    return lb


def _hgrn_qkg(hq, hf, lb):
    sig = _sigmoid(hf)
    q = _silu(hq)
    g = jnp.log(lb + (1.0 - lb) * sig)
    k = (1.0 - lb) * _sigmoid(-hf)
    return q, k, g


def _hgrn_prompt_kernel(hq_ref, hf_ref, hi_ref, hg_ref, lbl_ref, gn_ref, br_ref, s_ref, st_ref, *, l):
    t = pl.program_id(2)

    @pl.when(t == 0)
    def _():
        st_ref[...] = jnp.zeros_like(st_ref)

    q, k, g = _hgrn_qkg(hq_ref[...], hf_ref[...], _hgrn_lb(lbl_ref[...], l))
    o, st_new = _gated_block(q, k, g, hi_ref[...], st_ref[...])
    st_ref[...] = st_new
    br_ref[...] = (_head_rms(o, gn_ref[...]) * _silu(hg_ref[...])).astype(br_ref.dtype)

    @pl.when(t == pl.num_programs(2) - 1)
    def _():
        s_ref[...] = st_new.T


def _gla_prompt_kernel(gq_ref, gk_ref, gv_ref, gg_ref, la_ref, gn_ref, br_ref, s_ref, st_ref):
    t = pl.program_id(2)

    @pl.when(t == 0)
    def _():
        st_ref[...] = jnp.zeros_like(st_ref)

    o, st_new = _gated_block(gq_ref[...] * GLA_DK ** -0.5, gk_ref[...], la_ref[...], gv_ref[...], st_ref[...])
    st_ref[...] = st_new
    br_ref[...] = (_head_rms(o, gn_ref[...]) * _silu(gg_ref[...])).astype(br_ref.dtype)

    @pl.when(t == pl.num_programs(2) - 1)
    def _():
        s_ref[...] = st_new.T


def _scan_vmem(L, K, V):
    return (24 * L * L + 40 * L * K + 16 * L * V + 8 * K * V) * 4 + (8 << 20)


def _hgrn_prompt(z, lb_logits, norm_g3, l, bp, t_len):
    L = min(SCAN_BLOCK, t_len)
    nt = t_len // L
    K = V = HG_DK

    def col(off, w):
        return lambda b, h, t: (b * nt + t, off // w + h)

    return pl.pallas_call(
        functools.partial(_hgrn_prompt_kernel, l=l),
        grid=(bp, HG_HEADS, nt),
        in_specs=[pl.BlockSpec((L, K), col(OFF_HQ, K)), pl.BlockSpec((L, K), col(OFF_HF, K)),
                  pl.BlockSpec((L, V), col(OFF_HI, V)), pl.BlockSpec((L, V), col(OFF_HG, V)),
                  pl.BlockSpec((DEPTH, K), lambda b, h, t: (0, h)),
                  pl.BlockSpec((None, 1, V), lambda b, h, t: (l, 0, h))],
        out_specs=[pl.BlockSpec((L, V), lambda b, h, t: (b * nt + t, h)),
                   pl.BlockSpec((None, None, K, V), lambda b, h, t: (b, h, 0, 0))],
        out_shape=[jax.ShapeDtypeStruct((bp * t_len, HG_HEADS * V), BF16),
                   jax.ShapeDtypeStruct((bp, HG_HEADS, K, V), F32)],
        scratch_shapes=[pltpu.VMEM((V, K), F32)],
        compiler_params=_params(("parallel", "parallel", "arbitrary"), _scan_vmem(L, K, V)),
        name="hgrn_prompt",
    )(z, z, z, z, lb_logits, norm_g3)


def _gla_prompt(z, log_alpha, norm_g3, l, bp, t_len):
    L = min(SCAN_BLOCK, t_len)
    nt = t_len // L
    K, V = GLA_DK, GLA_DV

    def col(off, w):
        return lambda b, h, t: (b * nt + t, off // w + h)

    return pl.pallas_call(
        _gla_prompt_kernel,
        grid=(bp, GLA_HEADS, nt),
        in_specs=[pl.BlockSpec((L, K), col(OFF_GQ, K)), pl.BlockSpec((L, K), col(OFF_GK, K)),
                  pl.BlockSpec((L, V), col(OFF_GV, V)), pl.BlockSpec((L, V), col(OFF_GG, V)),
                  pl.BlockSpec((L, K), col(0, K)),
                  pl.BlockSpec((None, 1, V), lambda b, h, t: (l, 0, h))],
        out_specs=[pl.BlockSpec((L, V), lambda b, h, t: (b * nt + t, h)),
                   pl.BlockSpec((None, None, K, V), lambda b, h, t: (b, h, 0, 0))],
        out_shape=[jax.ShapeDtypeStruct((bp * t_len, GLA_HEADS * V), BF16),
                   jax.ShapeDtypeStruct((bp, GLA_HEADS, K, V), F32)],
        scratch_shapes=[pltpu.VMEM((V, K), F32)],
        compiler_params=_params(("parallel", "parallel", "arbitrary"), _scan_vmem(L, K, V)),
        name="gla_prompt",
    )(z, z, z, z, log_alpha, norm_g3)


def _rotate(x, cos, sin):
    half = x.shape[-1] // 2
    x1, x2 = x[:, :half], x[:, half:]
    return jnp.concatenate([x1 * cos - x2 * sin, x1 * sin + x2 * cos], axis=-1)


def _group_norm(o, g):
    c = o - jnp.mean(o, axis=-1, keepdims=True)
    return c * lax.rsqrt(jnp.mean(c * c, axis=-1, keepdims=True) + EPS) * g


def _ret_prompt_kernel(lg_ref, rq_ref, rk_ref, rv_ref, rg_ref, cos_ref, sin_ref, gn_ref, br_ref, s_ref, st_ref):
    h = pl.program_id(1)
    t = pl.program_id(2)
    lg = lg_ref[h]

    @pl.when(t == 0)
    def _():
        st_ref[...] = jnp.zeros_like(st_ref)

    cos, sin = cos_ref[...], sin_ref[...]
    q = _rotate(rq_ref[...], cos, sin)
    k = _rotate(rk_ref[...], cos, sin) * RET_DK ** -0.5
    v = rv_ref[...].astype(BF16)
    L, K = q.shape
    dist = (lax.broadcasted_iota(jnp.int32, (L, L), 0) - lax.broadcasted_iota(jnp.int32, (L, L), 1)).astype(F32)
    causal = dist >= 0
    decay = jnp.where(causal, jnp.exp(jnp.where(causal, dist * lg, 0.0)), 0.0)
    scores = _dot_nt(q.astype(BF16), k.astype(BF16)) * decay
    o = _dot(scores.astype(BF16), v)
    rowf = lax.broadcasted_iota(jnp.int32, (L, K), 0).astype(F32)
    st = st_ref[...]
    o = o + _dot((q * jnp.exp((rowf + 1.0) * lg)).astype(BF16), st.astype(BF16))
    k_out = k * jnp.exp((L - 1.0 - rowf) * lg)
    st_new = st * jnp.exp(jnp.full((1, 1), L, F32) * lg) + _dot(k_out.T.astype(BF16), v)
    st_ref[...] = st_new
    br_ref[...] = (_group_norm(o, gn_ref[...]) * _silu(rg_ref[...])).astype(br_ref.dtype)

    @pl.when(t == pl.num_programs(2) - 1)
    def _():
        s_ref[...] = st_new


def _ret_prompt(z, log_gamma, cos, sin, norm_g3, l, bp, t_len):
    L = min(SCAN_BLOCK, t_len)
    nt = t_len // L
    K, V = RET_DK, RET_DV

    def col(off, w):
        return lambda b, h, t: (b * nt + t, off // w + h)

    return pl.pallas_call(
        _ret_prompt_kernel,
        grid=(bp, RET_HEADS, nt),
        in_specs=[pl.BlockSpec(memory_space=pltpu.SMEM),
                  pl.BlockSpec((L, K), col(OFF_RQ, K)), pl.BlockSpec((L, K), col(OFF_RK, K)),
                  pl.BlockSpec((L, V), col(OFF_RV, V)), pl.BlockSpec((L, V), col(OFF_RG, V)),
                  pl.BlockSpec((L, K // 2), lambda b, h, t: (t, 0)),
                  pl.BlockSpec((L, K // 2), lambda b, h, t: (t, 0)),
                  pl.BlockSpec((None, 1, V), lambda b, h, t: (l, 0, h))],
        out_specs=[pl.BlockSpec((L, V), lambda b, h, t: (b * nt + t, h)),
                   pl.BlockSpec((None, None, K, V), lambda b, h, t: (b, h, 0, 0))],
        out_shape=[jax.ShapeDtypeStruct((bp * t_len, RET_HEADS * V), BF16),
                   jax.ShapeDtypeStruct((bp, RET_HEADS, K, V), F32)],
        scratch_shapes=[pltpu.VMEM((K, V), F32)],
        compiler_params=_params(("parallel", "parallel", "arbitrary"), _scan_vmem(L, K, V)),
        name="ret_prompt",
    )(log_gamma, z, z, z, z, cos, sin, norm_g3)


LRU_COLS = 256
CARRY_ROWS = SUBLANES


def _softplus(x):
    return jnp.maximum(x, 0.0) + jnp.log1p(jnp.exp(-jnp.abs(x)))


def _lru_gates(u, wa_ref, ba, wx_ref, bx, lam):
    nb = u.shape[1] // LRU_BW
    ra, rx = [], []
    for n in range(nb):
        un = u[:, n * LRU_BW:(n + 1) * LRU_BW].astype(BF16)
        ra.append(_dot(un, wa_ref[n].astype(BF16)))
        rx.append(_dot(un, wx_ref[n].astype(BF16)))
    r = _sigmoid(jnp.concatenate(ra, axis=1) + ba)
    i_g = _sigmoid(jnp.concatenate(rx, axis=1) + bx)
    log_a = -LRU_C * r * _softplus(-lam)
    a = jnp.exp(log_a)
    th = jnp.tanh(log_a)
    mult = jnp.sqrt(jnp.maximum(-2.0 * th / (1.0 - th), 0.0))
    return a, mult, i_g


def _lru_prompt_kernel(lx_ref, ly_ref, cw_ref, cb_ref, wa_ref, ba_ref, wx_ref, bx_ref, lam_ref,
                       br_ref, h_ref, c_ref, xbuf_ref, hprev_ref):
    t = pl.program_id(2)
    R = lx_ref.shape[0]

    @pl.when(t == 0)
    def _():
        xbuf_ref[0:CARRY_ROWS, :] = jnp.zeros((CARRY_ROWS, xbuf_ref.shape[1]), F32)
        hprev_ref[...] = jnp.zeros_like(hprev_ref)

    x = lx_ref[...]
    xbuf_ref[CARRY_ROWS:CARRY_ROWS + R, :] = x
    cw = cw_ref[...]
    u = cb_ref[...] + xbuf_ref[CARRY_ROWS - 3:CARRY_ROWS - 3 + R, :] * cw[0:1]
    u = u + xbuf_ref[CARRY_ROWS - 2:CARRY_ROWS - 2 + R, :] * cw[1:2]
    u = u + xbuf_ref[CARRY_ROWS - 1:CARRY_ROWS - 1 + R, :] * cw[2:3]
    u = u + x * cw[3:4]
    a, mult, i_g = _lru_gates(u, wa_ref, ba_ref[...], wx_ref, bx_ref[...], lam_ref[...])
    row = lax.broadcasted_iota(jnp.int32, (R, 1), 0)
    mult = jnp.where(row == jnp.where(t == 0, 0, -1), 1.0, mult)
    b = mult * i_g * u
    d = 1
    while d < R:
        valid = row >= d
        a_sh = pltpu.roll(a, d, 0)
        b_sh = pltpu.roll(b, d, 0)
        b = jnp.where(valid, a * b_sh + b, b)
        a = jnp.where(valid, a * a_sh, a)
        d *= 2
    hs = a * hprev_ref[0:1, :] + b
    br_ref[...] = (hs * jax.nn.gelu(ly_ref[...], approximate=True)).astype(br_ref.dtype)
    xbuf_ref[0:CARRY_ROWS, :] = x[R - CARRY_ROWS:R, :]
    hprev_ref[...] = jnp.broadcast_to(hs[R - 1:R, :], hprev_ref.shape)

    @pl.when(t == pl.num_programs(2) - 1)
    def _():
        h_ref[...] = hs[R - 1:R, :]
        c_ref[...] = x[R - (CONV_W - 1):R, :]


def _lru_prompt(z, P, l, bp, t_len):
    R = min(SCAN_BLOCK, t_len)
    nt = t_len // R
    C = LRU_COLS
    nc = LRU_WIDTH // C
    nb = C // LRU_BW
    vec = lambda: pl.BlockSpec((None, 1, C), lambda b, c, t: (l, 0, c))
    wblk = lambda: pl.BlockSpec((None, nb, LRU_BW, LRU_BW), lambda b, c, t: (l, c, 0, 0))
    return pl.pallas_call(
        _lru_prompt_kernel,
        grid=(bp, nc, nt),
        in_specs=[pl.BlockSpec((R, C), lambda b, c, t: (b * nt + t, OFF_LX // C + c)),
                  pl.BlockSpec((R, C), lambda b, c, t: (b * nt + t, OFF_LY // C + c)),
                  pl.BlockSpec((None, CONV_W, C), lambda b, c, t: (l, 0, c)),
                  vec(), wblk(), vec(), wblk(), vec(), vec()],
        out_specs=[pl.BlockSpec((R, C), lambda b, c, t: (b * nt + t, c)),
                   pl.BlockSpec((None, 1, C), lambda b, c, t: (b, 0, c)),
                   pl.BlockSpec((None, CONV_W - 1, C), lambda b, c, t: (b, 0, c))],
        out_shape=[jax.ShapeDtypeStruct((bp * t_len, LRU_WIDTH), BF16),
                   jax.ShapeDtypeStruct((bp, 1, LRU_WIDTH), F32),
                   jax.ShapeDtypeStruct((bp, CONV_W - 1, LRU_WIDTH), F32)],
        scratch_shapes=[pltpu.VMEM((R + CARRY_ROWS, C), F32), pltpu.VMEM((CARRY_ROWS, C), F32)],
        compiler_params=_params(("parallel", "parallel", "arbitrary"), 40 * R * C * 4 + (8 << 20)),
        name="lru_prompt",
    )(z, z, P['rglru_conv_w'], P['conv_b3'], P['rglru_w_a'], P['b_a3'], P['rglru_w_x'], P['b_x3'], P['lambda3'])


def _columns(rows):
    bb, K = rows.shape
    pad = jnp.concatenate([rows, jnp.zeros((LANES - bb, K), F32)], axis=0) if bb < LANES else rows
    return jnp.concatenate([pad[:, c * LANES:(c + 1) * LANES].T for c in range(K // LANES)], axis=0)


def _state_step(s_ref, so_ref, q, k, v, f_cols):
    bb, K = q.shape
    V = v.shape[1]
    k_cols = _columns(k)
    outs = []
    for i in range(bb):
        s = s_ref[i]
        kc = jnp.broadcast_to(k_cols[:, i:i + 1], (K, V))
        if f_cols.shape[0] == 1:
            s_new = s * f_cols + kc * v[i:i + 1, :]
        else:
            s_new = s * jnp.broadcast_to(f_cols[:, i:i + 1], (K, V)) + kc * v[i:i + 1, :]
        so_ref[i] = s_new
        qi = jnp.broadcast_to(q[i:i + 1, :], (SUBLANES, K)).astype(BF16)
        outs.append(_dot(qi, s_new.astype(BF16))[0:1, :])
    return jnp.concatenate(outs, axis=0)


def _hgrn_step_kernel(hq_ref, hf_ref, hi_ref, hg_ref, lbl_ref, gn_ref, s_ref, br_ref, so_ref, *, l):
    lb = _hgrn_lb(lbl_ref[...], l)
    q, k, g = _hgrn_qkg(hq_ref[...], hf_ref[...], lb)
    o = _state_step(s_ref, so_ref, q, k, hi_ref[...], _columns(jnp.exp(g)))
    br_ref[...] = (_head_rms(o, gn_ref[...]) * _silu(hg_ref[...])).astype(br_ref.dtype)


def _gla_step_kernel(gq_ref, gk_ref, gv_ref, gg_ref, la_ref, gn_ref, s_ref, br_ref, so_ref):
    o = _state_step(s_ref, so_ref, gq_ref[...] * GLA_DK ** -0.5, gk_ref[...], gv_ref[...],
                    _columns(jnp.exp(la_ref[...])))
    br_ref[...] = (_head_rms(o, gn_ref[...]) * _silu(gg_ref[...])).astype(br_ref.dtype)


def _ret_step_kernel(lg_ref, rq_ref, rk_ref, rv_ref, rg_ref, cos_ref, sin_ref, gn_ref, s_ref, br_ref, so_ref):
    lg = lg_ref[pl.program_id(0)]
    cos, sin = cos_ref[...], sin_ref[...]
    q = _rotate(rq_ref[...], cos, sin)
    k = _rotate(rk_ref[...], cos, sin) * RET_DK ** -0.5
    gamma = jnp.exp(jnp.full((1, 1), 1.0, F32) * lg)
    o = _state_step(s_ref, so_ref, q, k, rv_ref[...], gamma)
    br_ref[...] = (_group_norm(o, gn_ref[...]) * _silu(rg_ref[...])).astype(br_ref.dtype)


def _step_call(kernel, z, extra, extra_specs, state, new_states, l, row0, offs, K, V, H, bb, name, smem=None):
    bs = state.shape[1]
    rb = row0 // bb

    def col(off, w):
        return pl.BlockSpec((bb, w), lambda h, b: (rb + b, off // w + h))

    in_specs = [col(offs[0], K), col(offs[1], K), col(offs[2], V), col(offs[3], V)] + extra_specs
    in_specs.append(pl.BlockSpec((None, bb, None, K, V), lambda h, b: (l, b, h, 0, 0)))
    args = [z, z, z, z] + extra + [state]
    if smem is not None:
        in_specs = [pl.BlockSpec(memory_space=pltpu.SMEM)] + in_specs
        args = [smem] + args
    in_specs.append(pl.BlockSpec(memory_space=pl.ANY))
    aliases = {len(args): 1}
    return pl.pallas_call(
        functools.partial(_skip_ref, kernel, len(args)),
        grid=(H, bs // bb),
        in_specs=in_specs,
        out_specs=[pl.BlockSpec((bb, V), lambda h, b: (b, h)),
                   pl.BlockSpec((None, bb, None, K, V), lambda h, b: (l, b, h, 0, 0))],
        out_shape=[jax.ShapeDtypeStruct((bs, H * V), BF16),
                   jax.ShapeDtypeStruct(state.shape, F32)],
        input_output_aliases=aliases,
        compiler_params=_params(("parallel", "parallel"), 6 * bb * K * V * 4 + (8 << 20)),
        name=name,
    )(*args, new_states)


def _skip_ref(kernel, idx, *refs):
    return kernel(*refs[:idx], *refs[idx + 1:])


def _lru_step_kernel(lx_ref, ly_ref, cw_ref, cb_ref, wa_ref, ba_ref, wx_ref, bx_ref, lam_ref, c_ref, h_ref,
                     br_ref, ho_ref, co_ref):
    x = lx_ref[...]
    cw = cw_ref[...]
    u = cb_ref[...] + c_ref[0] * cw[0:1]
    u = u + c_ref[1] * cw[1:2]
    u = u + c_ref[2] * cw[2:3]
    u = u + x * cw[3:4]
    a, mult, i_g = _lru_gates(u, wa_ref, ba_ref[...], wx_ref, bx_ref[...], lam_ref[...])
    hs = a * h_ref[...] + mult * i_g * u
    br_ref[...] = (hs * jax.nn.gelu(ly_ref[...], approximate=True)).astype(br_ref.dtype)
    ho_ref[...] = hs
    co_ref[0] = c_ref[1]
    co_ref[1] = c_ref[2]
    co_ref[2] = x


def _lru_step(z, P, conv_state_t, h_state, l, row0):
    bs = h_state.shape[1]
    W = LRU_WIDTH
    vec = lambda: pl.BlockSpec((None, 1, W), lambda i: (l, 0, 0))
    wblk = lambda: pl.BlockSpec((None, LRU_BLOCKS, LRU_BW, LRU_BW), lambda i: (l, 0, 0, 0))
    return pl.pallas_call(
        _lru_step_kernel,
        grid=(1,),
        in_specs=[pl.BlockSpec((bs, W), lambda i: (row0 // bs, OFF_LX // W)),
                  pl.BlockSpec((bs, W), lambda i: (row0 // bs, OFF_LY // W)),
                  pl.BlockSpec((None, CONV_W, W), lambda i: (l, 0, 0)),
                  vec(), wblk(), vec(), wblk(), vec(), vec(),
                  pl.BlockSpec((None, CONV_W - 1, bs, W), lambda i: (l, 0, 0, 0)),
                  pl.BlockSpec((None, bs, W), lambda i: (l, 0, 0))],
        out_specs=[pl.BlockSpec((bs, W), lambda i: (0, 0)),
                   pl.BlockSpec((bs, W), lambda i: (0, 0)),
                   pl.BlockSpec((CONV_W - 1, bs, W), lambda i: (0, 0, 0))],
        out_shape=[jax.ShapeDtypeStruct((bs, W), BF16), jax.ShapeDtypeStruct((bs, W), F32),
                   jax.ShapeDtypeStruct((CONV_W - 1, bs, W), F32)],
        compiler_params=_params(("arbitrary",), 64 * bs * W * 4 + (8 << 20)),
        name="lru_step",
    )(z, z, P['rglru_conv_w'], P['conv_b3'], P['rglru_w_a'], P['b_a3'], P['rglru_w_x'], P['b_x3'], P['lambda3'],
      conv_state_t, h_state)


def _rope_tables(pos):
    half = RET_DK // 2
    freqs = jnp.exp(-math.log(ROPE_BASE) * jnp.arange(half, dtype=F32) / half)
    ang = pos.astype(F32)[:, None] * freqs[None]
    return jnp.cos(ang), jnp.sin(ang)


def _mixer(h, st, stacked, l, P, bp, t_len, bs):
    ns_hg, ns_rt, ns_gl = stacked
    mp = bp * t_len
    z = _mm_plain(h, P['w_in_t'], l, D_Z, 512, "w_in", transposed=True)
    log_alpha = _gla_gate(h, P['w_in_t'], P['w_up_pad'], P['b_up3'], l)
    gates = _gates(h, P['w_gate'], P['b_gate3'], l)

    a_p, s_hg_p = _hgrn_prompt(z, P['hgrn_lb_logits'], P['hgrn_g3'], l, bp, t_len)
    b_p, s_lh_p, s_lc_p = _lru_prompt(z, P, l, bp, t_len)
    c_p, s_rt_p = _ret_prompt(z, P['log_gamma'], P['cos_p'], P['sin_p'], P['ret_g3'], l, bp, t_len)
    d_p, s_gl_p = _gla_prompt(z, log_alpha, P['gla_g3'], l, bp, t_len)

    s_hg, s_lh, s_lc, s_rt, s_gl = st
    bb = 16
    a_s, s_hg_s = _step_call(
        functools.partial(_hgrn_step_kernel, l=l), z,
        [P['hgrn_lb_logits'], P['hgrn_g3']],
        [pl.BlockSpec((DEPTH, HG_DK), lambda h_, b: (0, h_)),
         pl.BlockSpec((None, 1, HG_DK), lambda h_, b: (l, 0, h_))],
        s_hg, ns_hg, l, mp, (OFF_HQ, OFF_HF, OFF_HI, OFF_HG), HG_DK, HG_DK, HG_HEADS, bb, "hgrn_step")
    b_s, s_lh_s, s_lc_s = _lru_step(z, P, s_lc, s_lh, l, mp)
    s_lc_s = jnp.swapaxes(s_lc_s, 0, 1)
    c_s, s_rt_s = _step_call(
        _ret_step_kernel, z,
        [P['cos_s'], P['sin_s'], P['ret_g3']],
        [pl.BlockSpec((1, RET_DK // 2), lambda h_, b: (0, 0)),
         pl.BlockSpec((1, RET_DK // 2), lambda h_, b: (0, 0)),
         pl.BlockSpec((None, 1, RET_DV), lambda h_, b: (l, 0, h_))],
        s_rt, ns_rt, l, mp, (OFF_RQ, OFF_RK, OFF_RV, OFF_RG), RET_DK, RET_DV, RET_HEADS, 8, "ret_step",
        smem=P['log_gamma'])
    d_s, s_gl_s = _step_call(
        _gla_step_kernel, z,
        [log_alpha, P['gla_g3']],
        [pl.BlockSpec((bb, GLA_DK), lambda h_, b: (mp // bb + b, h_)),
         pl.BlockSpec((None, 1, GLA_DV), lambda h_, b: (l, 0, h_))],
        s_gl, ns_gl, l, mp, (OFF_GQ, OFF_GK, OFF_GV, OFF_GG), GLA_DK, GLA_DV, GLA_HEADS, bb, "gla_step")

    cat = lambda p, s: jnp.concatenate([p, s], axis=0)
    merged = _merge(cat(a_p, a_s), cat(b_p, b_s), cat(c_p, c_s), cat(d_p, d_s), P['w_branch'], gates, l)
    out = _mm_plain(merged, P['w_out'], l, D_MODEL, 512, "w_out")
    new_p = (s_hg_p, s_lh_p.reshape(bp, LRU_WIDTH), s_lc_p, s_rt_p, s_gl_p)
    new_s = (s_hg_s, s_lh_s, s_lc_s, s_rt_s, s_gl_s)
    return out, new_p, new_s


def _layer(x, h, st, stacked, l, P, bp, t_len, bs, last):
    g = P['norm_g']
    f1 = _ffn_down(_ffn_up(h, P['w_ff_up'], l, 0), P['w_ff_down'], l, 0)
    x, h = _resid_norm(x, f1, g[l, 1:2], 0.5, g[l, 2:3])
    m, new_p, new_s = _mixer(h, st, stacked, l, P, bp, t_len, bs)
    x, h = _resid_norm(x, m, g[l, 3:4], 1.0, g[l, 4:5])
    f2 = _ffn_down(_ffn_up(h, P['w_ff_up'], l, 1), P['w_ff_down'], l, 1)
    x, h = _resid_norm(x, f2, g[l, 5:6], 0.5, None if last else g[l + 1, 0:1])
    return x, h, new_p, new_s


def kernel(x_prompt, x_sample, state_hgrn, state_rglru_h, state_rglru_conv, state_retention, state_gla,
           norm_g, w_ff_up, w_ff_down, w_in, hgrn_lb_logits, hgrn_norm_g,
           rglru_conv_w, rglru_conv_b, rglru_w_a, rglru_b_a, rglru_w_x, rglru_b_x, rglru_lambda,
           ret_norm_g, gla_w_up, gla_b_up, gla_norm_g, w_branch, w_gate, b_gate, w_out):
    bp, t_len, d = x_prompt.shape
    bs = x_sample.shape[0]
    depth = norm_g.shape[0]
    P = _prep(t_len, norm_g, w_ff_up, w_ff_down, w_in, hgrn_lb_logits, hgrn_norm_g,
              rglru_conv_w, rglru_conv_b, rglru_w_a, rglru_b_a, rglru_w_x, rglru_b_x, rglru_lambda,
              ret_norm_g, gla_w_up, gla_b_up, gla_norm_g, w_branch, w_gate, b_gate, w_out)
    st = (state_hgrn, state_rglru_h, jnp.swapaxes(state_rglru_conv, 1, 2), state_retention, state_gla)

    x = jnp.concatenate([x_prompt.reshape(bp * t_len, d), x_sample.reshape(bs, d)], axis=0)
    h = _rmsnorm_rows(x, norm_g[0, 0:1])
    new_p, new_s = [], []
    stacked = tuple(jnp.zeros_like(s) for s in (state_hgrn, state_retention, state_gla))
    for l in range(depth):
        x, h, stp, sts = _layer(x, h, st, stacked, l, P, bp, t_len, bs, l == depth - 1)
        stacked = (sts[0], sts[3], sts[4])
        new_p.append(stp)
        new_s.append(sts)
    yp = x[:bp * t_len].reshape(bp, t_len, d)
    ys = x[bp * t_len:].reshape(bs, 1, d)
    sp = [jnp.stack([s[i] for s in new_p]) for i in range(5)]
    s_lh = jnp.stack([s[1] for s in new_s])
    s_lc = jnp.stack([s[2] for s in new_s])
    return (yp, ys, sp[0], sp[1], sp[2], sp[3], sp[4], stacked[0], s_lh, s_lc, stacked[1], stacked[2])


def _prep(t_len, norm_g, w_ff_up, w_ff_down, w_in, hgrn_lb_logits, hgrn_norm_g,
          rglru_conv_w, rglru_conv_b, rglru_w_a, rglru_b_a, rglru_w_x, rglru_b_x, rglru_lambda,
          ret_norm_g, gla_w_up, gla_b_up, gla_norm_g, w_branch, w_gate, b_gate, w_out):
    row3 = lambda a: a.reshape(a.shape[0], 1, a.shape[1])
    cos_p, sin_p = _rope_tables(jnp.arange(t_len, dtype=jnp.int32))
    cos_s, sin_s = _rope_tables(PAST_LEN + jnp.arange(1, dtype=jnp.int32))
    P = dict(norm_g=norm_g, w_ff_up=w_ff_up, w_ff_down=w_ff_down, w_in_t=jnp.swapaxes(w_in, 1, 2),
             hgrn_lb_logits=hgrn_lb_logits, hgrn_g3=row3(hgrn_norm_g),
             rglru_conv_w=rglru_conv_w, conv_b3=row3(rglru_conv_b), rglru_w_a=rglru_w_a, b_a3=row3(rglru_b_a),
             rglru_w_x=rglru_w_x, b_x3=row3(rglru_b_x), lambda3=row3(rglru_lambda),
             ret_g3=row3(ret_norm_g), gla_g3=row3(gla_norm_g), b_up3=row3(gla_b_up),
             w_up_pad=jnp.pad(gla_w_up, ((0, 0), (0, LANES - GLA_RANK), (0, 0))),
             w_branch=w_branch, w_gate=w_gate, b_gate3=row3(b_gate), w_out=w_out,
             log_gamma=jnp.log1p(-jnp.exp2(-5.0 - jnp.arange(RET_HEADS, dtype=F32))),
             cos_p=cos_p, sin_p=sin_p, cos_s=cos_s, sin_s=sin_s)
    return P
```

```python
import functools
import math

import jax
import jax.numpy as jnp
from jax import lax
from jax.experimental import pallas as pl
from jax.experimental.pallas import tpu as pltpu

F32 = jnp.float32
BF16 = jnp.bfloat16

D_MODEL = 4096
DEPTH = 4
PAST_LEN = 16384
N_BRANCH = 4
BRANCH_WIDTH = D_MODEL // 4
HG_HEADS = 8
HG_DK = BRANCH_WIDTH // HG_HEADS
LRU_WIDTH = BRANCH_WIDTH
LRU_BLOCKS = 8
LRU_BW = LRU_WIDTH // LRU_BLOCKS
CONV_W = 4
LRU_C = 8.0
RET_HEADS = 4
RET_DK = BRANCH_WIDTH // RET_HEADS
RET_DV = 2 * RET_DK
ROPE_BASE = 10000.0
GLA_HEADS = 4
GLA_DK = BRANCH_WIDTH // 2 // GLA_HEADS
GLA_DV = BRANCH_WIDTH // GLA_HEADS
GLA_QK = GLA_HEADS * GLA_DK
GLA_RANK = 16
GLA_TAU = 16.0
D_FF = 256 * ((8 * D_MODEL // 3 + 255) // 256)
EPS = 1e-6

OFF_HQ, OFF_HF, OFF_HI, OFF_HG = 0, 1024, 2048, 3072
OFF_LX, OFF_LY = 4096, 5120
OFF_RQ, OFF_RK, OFF_RV, OFF_RG = 6144, 7168, 8192, 10240
OFF_GQ, OFF_GK, OFF_GV, OFF_GG = 12288, 12800, 13312, 14336
OFF_GLR = 15360
D_Z = OFF_GLR

LANES = 128
SUBLANES = 8
VMEM_LIMIT_CAP = 58 * 1024 * 1024
VMEM_SLACK = 12 * 1024 * 1024


def _params(semantics, vmem_bytes):
    return pltpu.CompilerParams(dimension_semantics=semantics,
                                vmem_limit_bytes=int(min(max(vmem_bytes, 16 << 20), VMEM_LIMIT_CAP)))


def _pick(n, candidates):
    for c in candidates:
        if n % c == 0:
            return c
    raise ValueError(f"no tile of {candidates} divides {n}")


def _sigmoid(x):
    return 1.0 / (1.0 + jnp.exp(-x))


def _silu(x):
    return x * _sigmoid(x)


def _dot(a, b):
    return jnp.dot(a, b, preferred_element_type=F32)


def _dot_nt(a, b):
    return lax.dot_general(a, b, (((1,), (1,)), ((), ())), preferred_element_type=F32)


def _rms(x, g):
    return x * lax.rsqrt(jnp.mean(x * x, axis=-1, keepdims=True) + EPS) * g


def _norm_kernel(x_ref, g_ref, h_ref):
    h_ref[...] = _rms(x_ref[...], g_ref[...]).astype(h_ref.dtype)


def _rmsnorm_rows(x, g_row):
    m, d = x.shape
    tm = _pick(m, (320, 160, 128, 64, 32, 16))
    return pl.pallas_call(
        _norm_kernel,
        grid=(m // tm,),
        in_specs=[pl.BlockSpec((tm, d), lambda i: (i, 0)), pl.BlockSpec((1, d), lambda i: (0, 0))],
        out_specs=pl.BlockSpec((tm, d), lambda i: (i, 0)),
        out_shape=jax.ShapeDtypeStruct((m, d), BF16),
        compiler_params=_params(("parallel",), tm * d * 24 + VMEM_SLACK),
        name="rmsnorm",
    )(x, g_row)


def _resid_norm_kernel(x_ref, f_ref, gp_ref, gn_ref, xo_ref, h_ref, *, scale):
    xn = x_ref[...] + scale * _rms(f_ref[...], gp_ref[...])
    xo_ref[...] = xn
    h_ref[...] = _rms(xn, gn_ref[...]).astype(h_ref.dtype)


def _resid_kernel(x_ref, f_ref, gp_ref, xo_ref, *, scale):
    xo_ref[...] = x_ref[...] + scale * _rms(f_ref[...], gp_ref[...])


def _resid_norm(x, f, g_post, scale, g_next):
    m, d = x.shape
    tm = _pick(m, (160, 128, 64, 32, 16))
    row = pl.BlockSpec((tm, d), lambda i: (i, 0))
    vec = pl.BlockSpec((1, d), lambda i: (0, 0))
    if g_next is None:
        return pl.pallas_call(
            functools.partial(_resid_kernel, scale=scale),
            grid=(m // tm,), in_specs=[row, row, vec], out_specs=row,
            out_shape=jax.ShapeDtypeStruct((m, d), F32),
            compiler_params=_params(("parallel",), tm * d * 40 + VMEM_SLACK),
            name="resid",
        )(x, f, g_post), None
    return pl.pallas_call(
        functools.partial(_resid_norm_kernel, scale=scale),
        grid=(m // tm,), in_specs=[row, row, vec, vec], out_specs=[row, row],
        out_shape=[jax.ShapeDtypeStruct((m, d), F32), jax.ShapeDtypeStruct((m, d), BF16)],
        compiler_params=_params(("parallel",), tm * d * 48 + VMEM_SLACK),
        name="resid_norm",
    )(x, f, g_post, g_next)


def _mm_tiles(m):
    return _pick(m, (2080, 1040, 640, 320, 128))


def _mm_swiglu_kernel(x_ref, wg_ref, wu_ref, o_ref):
    x = x_ref[...]
    g = _dot(x, wg_ref[...].astype(BF16))
    u = _dot(x, wu_ref[...].astype(BF16))
    o_ref[...] = (_silu(g) * u).astype(o_ref.dtype)


def _ffn_up(h, w_up, l, s):
    m, d = h.shape
    tm, tn = _mm_tiles(m), 256
    noff = D_FF // tn
    return pl.pallas_call(
        _mm_swiglu_kernel,
        grid=(m // tm, D_FF // tn),
        in_specs=[pl.BlockSpec((tm, d), lambda i, j: (i, 0), pipeline_mode=pl.Buffered(1)),
                  pl.BlockSpec((None, None, d, tn), lambda i, j: (l, s, 0, j)),
                  pl.BlockSpec((None, None, d, tn), lambda i, j: (l, s, 0, j + noff))],
        out_specs=pl.BlockSpec((tm, tn), lambda i, j: (i, j)),
        out_shape=jax.ShapeDtypeStruct((m, D_FF), BF16),
        compiler_params=_params(("parallel", "arbitrary"),
                                tm * d * 2 + 4 * d * tn * 4 + 2 * d * tn * 2 + 6 * tm * tn * 4 + VMEM_SLACK),
        name="ffn_up",
    )(h, w_up, w_up)


def _mm_acc_kernel(x_ref, w_ref, xr_ref, wr_ref, o_ref):
    k = pl.program_id(2)
    last = pl.num_programs(2) - 1

    @pl.when(k == 0)
    def _():
        o_ref[...] = _dot(x_ref[...], w_ref[...].astype(BF16))

    @pl.when(jnp.logical_and(k > 0, k < last))
    def _():
        o_ref[...] += _dot(x_ref[...], w_ref[...].astype(BF16))

    @pl.when(k == last)
    def _():
        o_ref[...] += _dot(xr_ref[...], wr_ref[...].astype(BF16))


FFN_DOWN_TK = 768
FFN_DOWN_REM = 256


def _ffn_down(a, w_down, l, s):
    m, kdim = a.shape
    n = w_down.shape[-1]
    tm, tn, tk, tr = _mm_tiles(m), 1024, FFN_DOWN_TK, FFN_DOWN_REM
    nfull = kdim // tk
    assert kdim - nfull * tk == tr and (nfull * tk) % tr == 0
    rblk = nfull * tk // tr
    return pl.pallas_call(
        _mm_acc_kernel,
        grid=(m // tm, n // tn, nfull + 1),
        in_specs=[pl.BlockSpec((tm, tk), lambda i, j, k: (i, jnp.minimum(k, nfull - 1))),
                  pl.BlockSpec((None, None, tk, tn), lambda i, j, k: (l, s, jnp.minimum(k, nfull - 1), j)),
                  pl.BlockSpec((tm, tr), lambda i, j, k: (i, rblk)),
                  pl.BlockSpec((None, None, tr, tn), lambda i, j, k: (l, s, rblk, j))],
        out_specs=pl.BlockSpec((tm, tn), lambda i, j, k: (i, j)),
        out_shape=jax.ShapeDtypeStruct((m, n), F32),
        compiler_params=_params(("parallel", "parallel", "arbitrary"),
                                3 * tm * tn * 4 + 2 * tm * (tk + tr) * 2 + 2 * (tk + tr) * tn * 4 + tk * tn * 2
                                + VMEM_SLACK),
        name="ffn_down",
    )(a, w_down, a, w_down)


def _mm_plain_kernel(x_ref, w_ref, o_ref):
    o_ref[...] = _dot(x_ref[...], w_ref[...].astype(BF16)).astype(o_ref.dtype)


def _mm_plain_nt_kernel(x_ref, wt_ref, o_ref):
    o_ref[...] = _dot_nt(x_ref[...], wt_ref[...].astype(BF16)).astype(o_ref.dtype)


def _mm_plain(x, w, l, n, tn, name, transposed=False):
    m, d = x.shape
    tm = _mm_tiles(m)
    w_spec = (pl.BlockSpec((None, tn, d), lambda i, j: (l, j, 0)) if transposed
              else pl.BlockSpec((None, d, tn), lambda i, j: (l, 0, j)))
    return pl.pallas_call(
        _mm_plain_nt_kernel if transposed else _mm_plain_kernel,
        grid=(m // tm, n // tn),
        in_specs=[pl.BlockSpec((tm, d), lambda i, j: (i, 0), pipeline_mode=pl.Buffered(1)), w_spec],
        out_specs=pl.BlockSpec((tm, tn), lambda i, j: (i, j)),
        out_shape=jax.ShapeDtypeStruct((m, n), F32),
        compiler_params=_params(("parallel", "arbitrary"),
                                tm * d * 2 + 2 * d * tn * 4 + d * tn * 2 + 4 * tm * tn * 4 + VMEM_SLACK),
        name=name,
    )(x, w)


def _mm_gate_kernel(x_ref, w_ref, b_ref, o_ref):
    o_ref[...] = _sigmoid(_dot(x_ref[...], w_ref[...].astype(BF16)) + b_ref[...]).astype(o_ref.dtype)


def _gates(h, w_gate, b_gate3, l):
    m, d = h.shape
    n = w_gate.shape[-1]
    tm, tn = _mm_tiles(m), 512
    return pl.pallas_call(
        _mm_gate_kernel,
        grid=(m // tm, n // tn),
        in_specs=[pl.BlockSpec((tm, d), lambda i, j: (i, 0), pipeline_mode=pl.Buffered(1)),
                  pl.BlockSpec((None, d, tn), lambda i, j: (l, 0, j)),
                  pl.BlockSpec((None, 1, tn), lambda i, j: (l, 0, j))],
        out_specs=pl.BlockSpec((tm, tn), lambda i, j: (i, j)),
        out_shape=jax.ShapeDtypeStruct((m, n), BF16),
        compiler_params=_params(("parallel", "arbitrary"),
                                tm * d * 2 + 2 * d * tn * 4 + d * tn * 2 + 4 * tm * tn * 4 + VMEM_SLACK),
        name="gates",
    )(h, w_gate, b_gate3)


def _merge_kernel(ba_ref, bb_ref, bc_ref, bd_ref, w_ref, g0_ref, g1_ref, g2_ref, g3_ref, o_ref):
    r = 0
    out = None
    for x_ref, g_ref in ((ba_ref, g0_ref), (bb_ref, g1_ref), (bc_ref, g2_ref), (bd_ref, g3_ref)):
        kw = x_ref.shape[1]
        term = g_ref[...].astype(F32) * _dot(x_ref[...], w_ref[r:r + kw, :].astype(BF16))
        out = term if out is None else out + term
        r += kw
    o_ref[...] = out.astype(o_ref.dtype)


def _merge(br_a, br_b, br_c, br_d, w_branch, gates, l):
    m = br_a.shape[0]
    kall = w_branch.shape[1]
    tm, tn = _pick(m, (1040, 640, 320, 128)), 256
    ncol = D_MODEL // tn

    def br_spec(a):
        return pl.BlockSpec((tm, a.shape[1]), lambda i, j: (i, 0), pipeline_mode=pl.Buffered(1))

    def gate_spec(nb):
        return pl.BlockSpec((tm, tn), lambda i, j: (i, nb * ncol + j))

    return pl.pallas_call(
        _merge_kernel,
        grid=(m // tm, ncol),
        in_specs=[br_spec(br_a), br_spec(br_b), br_spec(br_c), br_spec(br_d),
                  pl.BlockSpec((None, kall, tn), lambda i, j: (l, 0, j)),
                  gate_spec(0), gate_spec(1), gate_spec(2), gate_spec(3)],
        out_specs=pl.BlockSpec((tm, tn), lambda i, j: (i, j)),
        out_shape=jax.ShapeDtypeStruct((m, D_MODEL), BF16),
        compiler_params=_params(("parallel", "arbitrary"),
                                tm * kall * 2 + 2 * kall * tn * 4 + kall * tn * 2 + 10 * tm * tn * 2
                                + 8 * tm * tn * 4 + VMEM_SLACK),
        name="merge",
    )(br_a, br_b, br_c, br_d, w_branch, gates, gates, gates, gates)


def _log_sigmoid(x):
    return jnp.minimum(x, 0.0) - jnp.log1p(jnp.exp(-jnp.abs(x)))


def _gla_gate_kernel(h_ref, wlr_ref, wup_ref, b_ref, o_ref):
    row = lax.broadcasted_iota(jnp.int32, wlr_ref.shape, 0)
    wlr = jnp.where(row < GLA_RANK, wlr_ref[...], 0.0)
    glr = _dot_nt(h_ref[...], wlr.astype(BF16))
    pre = _dot(glr.astype(BF16), wup_ref[...].astype(BF16)) + b_ref[...]
    o_ref[...] = _log_sigmoid(pre) / GLA_TAU


def _gla_gate(h, w_in_t, w_up, b_up, l):
    m, d = h.shape
    tm = _pick(m, (640, 320, 128))
    return pl.pallas_call(
        _gla_gate_kernel,
        grid=(m // tm,),
        in_specs=[pl.BlockSpec((tm, d), lambda i: (i, 0)),
                  pl.BlockSpec((None, LANES, d), lambda i: (l, OFF_GLR // LANES, 0)),
                  pl.BlockSpec((None, LANES, GLA_QK), lambda i: (l, 0, 0)),
                  pl.BlockSpec((None, 1, GLA_QK), lambda i: (l, 0, 0))],
        out_specs=pl.BlockSpec((tm, GLA_QK), lambda i: (i, 0)),
        out_shape=jax.ShapeDtypeStruct((m, GLA_QK), F32),
        compiler_params=_params(("parallel",), 2 * tm * d * 2 + 3 * d * LANES * 4 + 6 * tm * GLA_QK * 4 + VMEM_SLACK),
        name="gla_gate",
    )(h, w_in_t, w_up, b_up)


SCAN_BLOCK = 256
HGRN_HEADS_PER_STEP = 2
GLA_HEADS_PER_STEP = 2


def _mid_rows(G, m):
    L, K = G.shape
    if 2 * m >= SUBLANES:
        G3 = G.reshape(L // (2 * m), 2 * m, K)
        return jnp.broadcast_to(G3[:, m - 1:m, :], G3.shape).reshape(L, K)
    assert m == 2
    G3 = G.reshape(L // SUBLANES, SUBLANES, K)
    sub = lax.broadcasted_iota(jnp.int32, G3.shape, 1)
    lo = jnp.broadcast_to(G3[:, 1:2, :], G3.shape)
    hi = jnp.broadcast_to(G3[:, 5:6, :], G3.shape)
    return jnp.where(sub < 4, lo, hi).reshape(L, K)


def _cumsum_rows(g):
    L = g.shape[0]
    tri = (lax.broadcasted_iota(jnp.int32, (L, L), 0) >= lax.broadcasted_iota(jnp.int32, (L, L), 1)).astype(BF16)
    hi = g.astype(BF16)
    r1 = g - hi.astype(F32)
    mid = r1.astype(BF16)
    lo = (r1 - mid.astype(F32)).astype(BF16)
    return _dot(tri, hi) + _dot(tri, mid) + _dot(tri, lo)


def _gated_block(q, k, g, v, st):
    L, K = q.shape
    G = _cumsum_rows(g)
    row = lax.broadcasted_iota(jnp.int32, (L, 1), 0)
    x = lax.broadcasted_iota(jnp.int32, (L, L), 0) ^ lax.broadcasted_iota(jnp.int32, (L, L), 1)
    scores = jnp.where(x == 0, _dot_nt(q.astype(BF16), k.astype(BF16)), 0.0)
    m = L // 2
    while m >= 1:
        second = (row & m) != 0
        if m == 1:
            e = jnp.where(second, g, 0.0)
        else:
            gm = _mid_rows(G, m)
            e = jnp.where(second, G - gm, gm - G)
        sc = jnp.exp(e)
        qs = jnp.where(second, q * sc, 0.0).astype(BF16)
        ks = jnp.where(second, 0.0, k * sc).astype(BF16)
        s = _dot_nt(qs, ks)
        scores = scores + (s if 2 * m == L else jnp.where(x < 2 * m, s, 0.0))
        m //= 2
    o = _dot(scores.astype(BF16), v.astype(BF16))
    o = o + _dot_nt((q * jnp.exp(G)).astype(BF16), st.astype(BF16))
    g_last = G[L - 1:L, :]
    k_out = (k * jnp.exp(g_last - G)).astype(BF16)
    st_new = st * jnp.exp(g_last) + _dot(v.T.astype(BF16), k_out)
    return o, st_new


def _head_rms(o, g):
    return o * lax.rsqrt(jnp.mean(o * o, axis=-1, keepdims=True) + EPS) * g


def _hgrn_lb(logits, l):
    e = jnp.exp(logits - jnp.max(logits, axis=0, keepdims=True))
    p = e / jnp.sum(e, axis=0, keepdims=True)
    lb = jnp.zeros_like(p[0:1])
    for i in range(1, l + 1):
        lb = lb + p[i:i + 1]
    return lb


def _hgrn_qkg(hq, hf, lb):
    sig = _sigmoid(hf)
    q = _silu(hq)
    g = jnp.log(lb + (1.0 - lb) * sig)
    k = (1.0 - lb) * _sigmoid(-hf)
    return q, k, g


def _hgrn_prompt_kernel(hq_ref, hf_ref, hi_ref, hg_ref, lbl_ref, gn_ref, br_ref, s_ref, st_ref, *, l):
    t = pl.program_id(2)

    @pl.when(t == 0)
    def _():
        st_ref[...] = jnp.zeros_like(st_ref)

    lb = _hgrn_lb(lbl_ref[...], l)
    new_states = []
    for hh in range(st_ref.shape[0]):
        c = slice(hh * HG_DK, (hh + 1) * HG_DK)
        q, k, g = _hgrn_qkg(hq_ref[:, c], hf_ref[:, c], lb[:, c])
        o, st_new = _gated_block(q, k, g, hi_ref[:, c], st_ref[hh])
        st_ref[hh] = st_new
        br_ref[:, c] = (_head_rms(o, gn_ref[:, c]) * _silu(hg_ref[:, c])).astype(br_ref.dtype)
        new_states.append(st_new)

    @pl.when(t == pl.num_programs(2) - 1)
    def _():
        for hh, st_new in enumerate(new_states):
            s_ref[hh] = st_new.T


def _gla_prompt_kernel(gq_ref, gk_ref, gv_ref, gg_ref, la_ref, gn_ref, br_ref, s_ref, st_ref):
    t = pl.program_id(2)

    @pl.when(t == 0)
    def _():
        st_ref[...] = jnp.zeros_like(st_ref)

    new_states = []
    for hh in range(st_ref.shape[0]):
        ck = slice(hh * GLA_DK, (hh + 1) * GLA_DK)
        cv = slice(hh * GLA_DV, (hh + 1) * GLA_DV)
        o, st_new = _gated_block(gq_ref[:, ck] * GLA_DK ** -0.5, gk_ref[:, ck], la_ref[:, ck], gv_ref[:, cv],
                                 st_ref[hh])
        st_ref[hh] = st_new
        br_ref[:, cv] = (_head_rms(o, gn_ref[:, cv]) * _silu(gg_ref[:, cv])).astype(br_ref.dtype)
        new_states.append(st_new)

    @pl.when(t == pl.num_programs(2) - 1)
    def _():
        for hh, st_new in enumerate(new_states):
            s_ref[hh] = st_new.T


def _scan_vmem(L, K, V):
    return (24 * L * L + 40 * L * K + 16 * L * V + 8 * K * V) * 4 + (8 << 20)


def _hgrn_prompt(z, lb_logits, norm_g3, l, bp, t_len):
    L = min(SCAN_BLOCK, t_len)
    nt = t_len // L
    K = V = HG_DK
    hp = HGRN_HEADS_PER_STEP
    W = hp * K

    def col(off):
        return pl.BlockSpec((L, W), lambda b, h, t: (b * nt + t, off // W + h))

    return pl.pallas_call(
        functools.partial(_hgrn_prompt_kernel, l=l),
        grid=(bp, HG_HEADS // hp, nt),
        in_specs=[col(OFF_HQ), col(OFF_HF), col(OFF_HI), col(OFF_HG),
                  pl.BlockSpec((DEPTH, W), lambda b, h, t: (0, h)),
                  pl.BlockSpec((None, 1, W), lambda b, h, t: (l, 0, h))],
        out_specs=[pl.BlockSpec((L, W), lambda b, h, t: (b * nt + t, h)),
                   pl.BlockSpec((None, hp, K, V), lambda b, h, t: (b, h, 0, 0))],
        out_shape=[jax.ShapeDtypeStruct((bp * t_len, HG_HEADS * V), BF16),
                   jax.ShapeDtypeStruct((bp, HG_HEADS, K, V), F32)],
        scratch_shapes=[pltpu.VMEM((hp, V, K), F32)],
        compiler_params=_params(("parallel", "parallel", "arbitrary"), hp * _scan_vmem(L, K, V)),
        name="hgrn_prompt",
    )(z, z, z, z, lb_logits, norm_g3)


def _gla_prompt(z, log_alpha, norm_g3, l, bp, t_len):
    L = min(SCAN_BLOCK, t_len)
    nt = t_len // L
    K, V = GLA_DK, GLA_DV
    hp = GLA_HEADS_PER_STEP

    def col(off, w):
        return pl.BlockSpec((L, hp * w), lambda b, h, t: (b * nt + t, off // (hp * w) + h))

    return pl.pallas_call(
        _gla_prompt_kernel,
        grid=(bp, GLA_HEADS // hp, nt),
        in_specs=[col(OFF_GQ, K), col(OFF_GK, K), col(OFF_GV, V), col(OFF_GG, V), col(0, K),
                  pl.BlockSpec((None, 1, hp * V), lambda b, h, t: (l, 0, h))],
        out_specs=[pl.BlockSpec((L, hp * V), lambda b, h, t: (b * nt + t, h)),
                   pl.BlockSpec((None, hp, K, V), lambda b, h, t: (b, h, 0, 0))],
        out_shape=[jax.ShapeDtypeStruct((bp * t_len, GLA_HEADS * V), BF16),
                   jax.ShapeDtypeStruct((bp, GLA_HEADS, K, V), F32)],
        scratch_shapes=[pltpu.VMEM((hp, V, K), F32)],
        compiler_params=_params(("parallel", "parallel", "arbitrary"), hp * _scan_vmem(L, K, V)),
        name="gla_prompt",
    )(z, z, z, z, log_alpha, norm_g3)


def _rotate(x, cos, sin):
    half = x.shape[-1] // 2
    x1, x2 = x[:, :half], x[:, half:]
    return jnp.concatenate([x1 * cos - x2 * sin, x1 * sin + x2 * cos], axis=-1)


def _group_norm(o, g):
    c = o - jnp.mean(o, axis=-1, keepdims=True)
    return c * lax.rsqrt(jnp.mean(c * c, axis=-1, keepdims=True) + EPS) * g


def _ret_prompt_kernel(lg_ref, rq_ref, rk_ref, rv_ref, rg_ref, cos_ref, sin_ref, gn_ref, br_ref, s_ref, st_ref):
    h = pl.program_id(1)
    t = pl.program_id(2)
    lg = lg_ref[h]

    @pl.when(t == 0)
    def _():
        st_ref[...] = jnp.zeros_like(st_ref)

    cos, sin = cos_ref[...], sin_ref[...]
    q = _rotate(rq_ref[...], cos, sin)
    k = _rotate(rk_ref[...], cos, sin) * RET_DK ** -0.5
    v = rv_ref[...].astype(BF16)
    L, K = q.shape
    dist = (lax.broadcasted_iota(jnp.int32, (L, L), 0) - lax.broadcasted_iota(jnp.int32, (L, L), 1)).astype(F32)
    causal = dist >= 0
    decay = jnp.where(causal, jnp.exp(jnp.where(causal, dist * lg, 0.0)), 0.0)
    scores = _dot_nt(q.astype(BF16), k.astype(BF16)) * decay
    o = _dot(scores.astype(BF16), v)
    rowf = lax.broadcasted_iota(jnp.int32, (L, K), 0).astype(F32)
    st = st_ref[...]
    o = o + _dot((q * jnp.exp((rowf + 1.0) * lg)).astype(BF16), st.astype(BF16))
    k_out = k * jnp.exp((L - 1.0 - rowf) * lg)
    st_new = st * jnp.exp(jnp.full((1, 1), L, F32) * lg) + _dot(k_out.T.astype(BF16), v)
    st_ref[...] = st_new
    br_ref[...] = (_group_norm(o, gn_ref[...]) * _silu(rg_ref[...])).astype(br_ref.dtype)

    @pl.when(t == pl.num_programs(2) - 1)
    def _():
        s_ref[...] = st_new


def _ret_prompt(z, log_gamma, cos, sin, norm_g3, l, bp, t_len):
    L = min(SCAN_BLOCK, t_len)
    nt = t_len // L
    K, V = RET_DK, RET_DV

    def col(off, w):
        return lambda b, h, t: (b * nt + t, off // w + h)

    return pl.pallas_call(
        _ret_prompt_kernel,
        grid=(bp, RET_HEADS, nt),
        in_specs=[pl.BlockSpec(memory_space=pltpu.SMEM),
                  pl.BlockSpec((L, K), col(OFF_RQ, K)), pl.BlockSpec((L, K), col(OFF_RK, K)),
                  pl.BlockSpec((L, V), col(OFF_RV, V)), pl.BlockSpec((L, V), col(OFF_RG, V)),
                  pl.BlockSpec((L, K // 2), lambda b, h, t: (t, 0)),
                  pl.BlockSpec((L, K // 2), lambda b, h, t: (t, 0)),
                  pl.BlockSpec((None, 1, V), lambda b, h, t: (l, 0, h))],
        out_specs=[pl.BlockSpec((L, V), lambda b, h, t: (b * nt + t, h)),
                   pl.BlockSpec((None, None, K, V), lambda b, h, t: (b, h, 0, 0))],
        out_shape=[jax.ShapeDtypeStruct((bp * t_len, RET_HEADS * V), BF16),
                   jax.ShapeDtypeStruct((bp, RET_HEADS, K, V), F32)],
        scratch_shapes=[pltpu.VMEM((K, V), F32)],
        compiler_params=_params(("parallel", "parallel", "arbitrary"), _scan_vmem(L, K, V)),
        name="ret_prompt",
    )(log_gamma, z, z, z, z, cos, sin, norm_g3)


LRU_COLS = 256
CARRY_ROWS = SUBLANES


def _softplus(x):
    return jnp.maximum(x, 0.0) + jnp.log1p(jnp.exp(-jnp.abs(x)))


def _lru_gates(u, wa_ref, ba, wx_ref, bx, lam):
    nb = u.shape[1] // LRU_BW
    ra, rx = [], []
    for n in range(nb):
        un = u[:, n * LRU_BW:(n + 1) * LRU_BW].astype(BF16)
        ra.append(_dot(un, wa_ref[n].astype(BF16)))
        rx.append(_dot(un, wx_ref[n].astype(BF16)))
    r = _sigmoid(jnp.concatenate(ra, axis=1) + ba)
    i_g = _sigmoid(jnp.concatenate(rx, axis=1) + bx)
    log_a = -LRU_C * r * _softplus(-lam)
    a = jnp.exp(log_a)
    th = jnp.tanh(log_a)
    mult = jnp.sqrt(jnp.maximum(-2.0 * th / (1.0 - th), 0.0))
    return a, mult, i_g


def _lru_prompt_kernel(lx_ref, ly_ref, cw_ref, cb_ref, wa_ref, ba_ref, wx_ref, bx_ref, lam_ref,
                       br_ref, h_ref, c_ref, xbuf_ref, hprev_ref):
    t = pl.program_id(2)
    R = lx_ref.shape[0]

    @pl.when(t == 0)
    def _():
        xbuf_ref[0:CARRY_ROWS, :] = jnp.zeros((CARRY_ROWS, xbuf_ref.shape[1]), F32)
        hprev_ref[...] = jnp.zeros_like(hprev_ref)

    x = lx_ref[...]
    xbuf_ref[CARRY_ROWS:CARRY_ROWS + R, :] = x
    cw = cw_ref[...]
    u = cb_ref[...] + xbuf_ref[CARRY_ROWS - 3:CARRY_ROWS - 3 + R, :] * cw[0:1]
    u = u + xbuf_ref[CARRY_ROWS - 2:CARRY_ROWS - 2 + R, :] * cw[1:2]
    u = u + xbuf_ref[CARRY_ROWS - 1:CARRY_ROWS - 1 + R, :] * cw[2:3]
    u = u + x * cw[3:4]
    a, mult, i_g = _lru_gates(u, wa_ref, ba_ref[...], wx_ref, bx_ref[...], lam_ref[...])
    row = lax.broadcasted_iota(jnp.int32, (R, 1), 0)
    mult = jnp.where(row == jnp.where(t == 0, 0, -1), 1.0, mult)
    b = mult * i_g * u
    d = 1
    while d < R:
        valid = row >= d
        a_sh = pltpu.roll(a, d, 0)
        b_sh = pltpu.roll(b, d, 0)
        b = jnp.where(valid, a * b_sh + b, b)
        a = jnp.where(valid, a * a_sh, a)
        d *= 2
    hs = a * hprev_ref[0:1, :] + b
    br_ref[...] = (hs * jax.nn.gelu(ly_ref[...], approximate=True)).astype(br_ref.dtype)
    xbuf_ref[0:CARRY_ROWS, :] = x[R - CARRY_ROWS:R, :]
    hprev_ref[...] = jnp.broadcast_to(hs[R - 1:R, :], hprev_ref.shape)

    @pl.when(t == pl.num_programs(2) - 1)
    def _():
        h_ref[...] = hs[R - 1:R, :]
        c_ref[...] = x[R - (CONV_W - 1):R, :]


def _lru_prompt(z, P, l, bp, t_len):
    R = min(SCAN_BLOCK, t_len)
    nt = t_len // R
    C = LRU_COLS
    nc = LRU_WIDTH // C
    nb = C // LRU_BW
    vec = lambda: pl.BlockSpec((None, 1, C), lambda b, c, t: (l, 0, c))
    wblk = lambda: pl.BlockSpec((None, nb, LRU_BW, LRU_BW), lambda b, c, t: (l, c, 0, 0))
    return pl.pallas_call(
        _lru_prompt_kernel,
        grid=(bp, nc, nt),
        in_specs=[pl.BlockSpec((R, C), lambda b, c, t: (b * nt + t, OFF_LX // C + c)),
                  pl.BlockSpec((R, C), lambda b, c, t: (b * nt + t, OFF_LY // C + c)),
                  pl.BlockSpec((None, CONV_W, C), lambda b, c, t: (l, 0, c)),
                  vec(), wblk(), vec(), wblk(), vec(), vec()],
        out_specs=[pl.BlockSpec((R, C), lambda b, c, t: (b * nt + t, c)),
                   pl.BlockSpec((None, 1, C), lambda b, c, t: (b, 0, c)),
                   pl.BlockSpec((None, CONV_W - 1, C), lambda b, c, t: (b, 0, c))],
        out_shape=[jax.ShapeDtypeStruct((bp * t_len, LRU_WIDTH), BF16),
                   jax.ShapeDtypeStruct((bp, 1, LRU_WIDTH), F32),
                   jax.ShapeDtypeStruct((bp, CONV_W - 1, LRU_WIDTH), F32)],
        scratch_shapes=[pltpu.VMEM((R + CARRY_ROWS, C), F32), pltpu.VMEM((CARRY_ROWS, C), F32)],
        compiler_params=_params(("parallel", "parallel", "arbitrary"), 40 * R * C * 4 + (8 << 20)),
        name="lru_prompt",
    )(z, z, P['rglru_conv_w'], P['conv_b3'], P['rglru_w_a'], P['b_a3'], P['rglru_w_x'], P['b_x3'], P['lambda3'])


def _columns(rows):
    bb, K = rows.shape
    pad = jnp.concatenate([rows, jnp.zeros((LANES - bb, K), F32)], axis=0) if bb < LANES else rows
    return jnp.concatenate([pad[:, c * LANES:(c + 1) * LANES].T for c in range(K // LANES)], axis=0)


def _state_step(s_ref, so_ref, q, k, v, f_cols):
    bb, K = q.shape
    V = v.shape[1]
    k_cols = _columns(k)
    outs = []
    for i in range(bb):
        s = s_ref[i]
        kc = jnp.broadcast_to(k_cols[:, i:i + 1], (K, V))
        if f_cols.shape[0] == 1:
            s_new = s * f_cols + kc * v[i:i + 1, :]
        else:
            s_new = s * jnp.broadcast_to(f_cols[:, i:i + 1], (K, V)) + kc * v[i:i + 1, :]
        so_ref[i] = s_new
        qi = jnp.broadcast_to(q[i:i + 1, :], (SUBLANES, K)).astype(BF16)
        outs.append(_dot(qi, s_new.astype(BF16))[0:1, :])
    return jnp.concatenate(outs, axis=0)


def _hgrn_step_kernel(hq_ref, hf_ref, hi_ref, hg_ref, lbl_ref, gn_ref, s_ref, br_ref, so_ref, *, l):
    lb = _hgrn_lb(lbl_ref[...], l)
    q, k, g = _hgrn_qkg(hq_ref[...], hf_ref[...], lb)
    o = _state_step(s_ref, so_ref, q, k, hi_ref[...], _columns(jnp.exp(g)))
    br_ref[...] = (_head_rms(o, gn_ref[...]) * _silu(hg_ref[...])).astype(br_ref.dtype)


def _gla_step_kernel(gq_ref, gk_ref, gv_ref, gg_ref, la_ref, gn_ref, s_ref, br_ref, so_ref):
    o = _state_step(s_ref, so_ref, gq_ref[...] * GLA_DK ** -0.5, gk_ref[...], gv_ref[...],
                    _columns(jnp.exp(la_ref[...])))
    br_ref[...] = (_head_rms(o, gn_ref[...]) * _silu(gg_ref[...])).astype(br_ref.dtype)


def _ret_step_kernel(lg_ref, rq_ref, rk_ref, rv_ref, rg_ref, cos_ref, sin_ref, gn_ref, s_ref, br_ref, so_ref):
    lg = lg_ref[pl.program_id(0)]
    cos, sin = cos_ref[...], sin_ref[...]
    q = _rotate(rq_ref[...], cos, sin)
    k = _rotate(rk_ref[...], cos, sin) * RET_DK ** -0.5
    gamma = jnp.exp(jnp.full((1, 1), 1.0, F32) * lg)
    o = _state_step(s_ref, so_ref, q, k, rv_ref[...], gamma)
    br_ref[...] = (_group_norm(o, gn_ref[...]) * _silu(rg_ref[...])).astype(br_ref.dtype)


def _step_call(kernel, z, extra, extra_specs, state, new_states, l, row0, offs, K, V, H, bb, name, smem=None):
    bs = state.shape[1]
    rb = row0 // bb

    def col(off, w):
        return pl.BlockSpec((bb, w), lambda h, b: (rb + b, off // w + h))

    in_specs = [col(offs[0], K), col(offs[1], K), col(offs[2], V), col(offs[3], V)] + extra_specs
    in_specs.append(pl.BlockSpec((None, bb, None, K, V), lambda h, b: (l, b, h, 0, 0)))
    args = [z, z, z, z] + extra + [state]
    if smem is not None:
        in_specs = [pl.BlockSpec(memory_space=pltpu.SMEM)] + in_specs
        args = [smem] + args
    in_specs.append(pl.BlockSpec(memory_space=pl.ANY))
    aliases = {len(args): 1}
    return pl.pallas_call(
        functools.partial(_skip_ref, kernel, len(args)),
        grid=(H, bs // bb),
        in_specs=in_specs,
        out_specs=[pl.BlockSpec((bb, V), lambda h, b: (b, h)),
                   pl.BlockSpec((None, bb, None, K, V), lambda h, b: (l, b, h, 0, 0))],
        out_shape=[jax.ShapeDtypeStruct((bs, H * V), BF16),
                   jax.ShapeDtypeStruct(state.shape, F32)],
        input_output_aliases=aliases,
        compiler_params=_params(("parallel", "parallel"), 6 * bb * K * V * 4 + (8 << 20)),
        name=name,
    )(*args, new_states)


def _skip_ref(kernel, idx, *refs):
    return kernel(*refs[:idx], *refs[idx + 1:])


def _lru_step_kernel(lx_ref, ly_ref, cw_ref, cb_ref, wa_ref, ba_ref, wx_ref, bx_ref, lam_ref, c_ref, h_ref,
                     br_ref, ho_ref, co_ref):
    x = lx_ref[...]
    cw = cw_ref[...]
    u = cb_ref[...] + c_ref[0] * cw[0:1]
    u = u + c_ref[1] * cw[1:2]
    u = u + c_ref[2] * cw[2:3]
    u = u + x * cw[3:4]
    a, mult, i_g = _lru_gates(u, wa_ref, ba_ref[...], wx_ref, bx_ref[...], lam_ref[...])
    hs = a * h_ref[...] + mult * i_g * u
    br_ref[...] = (hs * jax.nn.gelu(ly_ref[...], approximate=True)).astype(br_ref.dtype)
    ho_ref[...] = hs
    co_ref[0] = c_ref[1]
    co_ref[1] = c_ref[2]
    co_ref[2] = x


def _lru_step(z, P, conv_state_t, h_state, l, row0):
    bs = h_state.shape[1]
    W = LRU_WIDTH
    vec = lambda: pl.BlockSpec((None, 1, W), lambda i: (l, 0, 0))
    wblk = lambda: pl.BlockSpec((None, LRU_BLOCKS, LRU_BW, LRU_BW), lambda i: (l, 0, 0, 0))
    return pl.pallas_call(
        _lru_step_kernel,
        grid=(1,),
        in_specs=[pl.BlockSpec((bs, W), lambda i: (row0 // bs, OFF_LX // W)),
                  pl.BlockSpec((bs, W), lambda i: (row0 // bs, OFF_LY // W)),
                  pl.BlockSpec((None, CONV_W, W), lambda i: (l, 0, 0)),
                  vec(), wblk(), vec(), wblk(), vec(), vec(),
                  pl.BlockSpec((None, CONV_W - 1, bs, W), lambda i: (l, 0, 0, 0)),
                  pl.BlockSpec((None, bs, W), lambda i: (l, 0, 0))],
        out_specs=[pl.BlockSpec((bs, W), lambda i: (0, 0)),
                   pl.BlockSpec((bs, W), lambda i: (0, 0)),
                   pl.BlockSpec((CONV_W - 1, bs, W), lambda i: (0, 0, 0))],
        out_shape=[jax.ShapeDtypeStruct((bs, W), BF16), jax.ShapeDtypeStruct((bs, W), F32),
                   jax.ShapeDtypeStruct((CONV_W - 1, bs, W), F32)],
        compiler_params=_params(("arbitrary",), 64 * bs * W * 4 + (8 << 20)),
        name="lru_step",
    )(z, z, P['rglru_conv_w'], P['conv_b3'], P['rglru_w_a'], P['b_a3'], P['rglru_w_x'], P['b_x3'], P['lambda3'],
      conv_state_t, h_state)


def _rope_tables(pos):
    half = RET_DK // 2
    freqs = jnp.exp(-math.log(ROPE_BASE) * jnp.arange(half, dtype=F32) / half)
    ang = pos.astype(F32)[:, None] * freqs[None]
    return jnp.cos(ang), jnp.sin(ang)


def _mixer(h, st, stacked, l, P, bp, t_len, bs):
    ns_hg, ns_rt, ns_gl = stacked
    mp = bp * t_len
    z = _mm_plain(h, P['w_in_t'], l, D_Z, 512, "w_in", transposed=True)
    log_alpha = _gla_gate(h, P['w_in_t'], P['w_up_pad'], P['b_up3'], l)
    gates = _gates(h, P['w_gate'], P['b_gate3'], l)

    a_p, s_hg_p = _hgrn_prompt(z, P['hgrn_lb_logits'], P['hgrn_g3'], l, bp, t_len)
    b_p, s_lh_p, s_lc_p = _lru_prompt(z, P, l, bp, t_len)
    c_p, s_rt_p = _ret_prompt(z, P['log_gamma'], P['cos_p'], P['sin_p'], P['ret_g3'], l, bp, t_len)
    d_p, s_gl_p = _gla_prompt(z, log_alpha, P['gla_g3'], l, bp, t_len)

    s_hg, s_lh, s_lc, s_rt, s_gl = st
    bb = 16
    a_s, s_hg_s = _step_call(
        functools.partial(_hgrn_step_kernel, l=l), z,
        [P['hgrn_lb_logits'], P['hgrn_g3']],
        [pl.BlockSpec((DEPTH, HG_DK), lambda h_, b: (0, h_)),
         pl.BlockSpec((None, 1, HG_DK), lambda h_, b: (l, 0, h_))],
        s_hg, ns_hg, l, mp, (OFF_HQ, OFF_HF, OFF_HI, OFF_HG), HG_DK, HG_DK, HG_HEADS, bb, "hgrn_step")
    b_s, s_lh_s, s_lc_s = _lru_step(z, P, s_lc, s_lh, l, mp)
    s_lc_s = jnp.swapaxes(s_lc_s, 0, 1)
    c_s, s_rt_s = _step_call(
        _ret_step_kernel, z,
        [P['cos_s'], P['sin_s'], P['ret_g3']],
        [pl.BlockSpec((1, RET_DK // 2), lambda h_, b: (0, 0)),
         pl.BlockSpec((1, RET_DK // 2), lambda h_, b: (0, 0)),
         pl.BlockSpec((None, 1, RET_DV), lambda h_, b: (l, 0, h_))],
        s_rt, ns_rt, l, mp, (OFF_RQ, OFF_RK, OFF_RV, OFF_RG), RET_DK, RET_DV, RET_HEADS, 8, "ret_step",
        smem=P['log_gamma'])
    d_s, s_gl_s = _step_call(
        _gla_step_kernel, z,
        [log_alpha, P['gla_g3']],
        [pl.BlockSpec((bb, GLA_DK), lambda h_, b: (mp // bb + b, h_)),
         pl.BlockSpec((None, 1, GLA_DV), lambda h_, b: (l, 0, h_))],
        s_gl, ns_gl, l, mp, (OFF_GQ, OFF_GK, OFF_GV, OFF_GG), GLA_DK, GLA_DV, GLA_HEADS, bb, "gla_step")

    cat = lambda p, s: jnp.concatenate([p, s], axis=0)
    merged = _merge(cat(a_p, a_s), cat(b_p, b_s), cat(c_p, c_s), cat(d_p, d_s), P['w_branch'], gates, l)
    out = _mm_plain(merged, P['w_out'], l, D_MODEL, 512, "w_out")
    new_p = (s_hg_p, s_lh_p.reshape(bp, LRU_WIDTH), s_lc_p, s_rt_p, s_gl_p)
    new_s = (s_hg_s, s_lh_s, s_lc_s, s_rt_s, s_gl_s)
    return out, new_p, new_s


def _layer(x, h, st, stacked, l, P, bp, t_len, bs, last):
    g = P['norm_g']
    f1 = _ffn_down(_ffn_up(h, P['w_ff_up'], l, 0), P['w_ff_down'], l, 0)
    x, h = _resid_norm(x, f1, g[l, 1:2], 0.5, g[l, 2:3])
    m, new_p, new_s = _mixer(h, st, stacked, l, P, bp, t_len, bs)
    x, h = _resid_norm(x, m, g[l, 3:4], 1.0, g[l, 4:5])
    f2 = _ffn_down(_ffn_up(h, P['w_ff_up'], l, 1), P['w_ff_down'], l, 1)
    x, h = _resid_norm(x, f2, g[l, 5:6], 0.5, None if last else g[l + 1, 0:1])
    return x, h, new_p, new_s


def kernel(x_prompt, x_sample, state_hgrn, state_rglru_h, state_rglru_conv, state_retention, state_gla,
           norm_g, w_ff_up, w_ff_down, w_in, hgrn_lb_logits, hgrn_norm_g,
           rglru_conv_w, rglru_conv_b, rglru_w_a, rglru_b_a, rglru_w_x, rglru_b_x, rglru_lambda,
           ret_norm_g, gla_w_up, gla_b_up, gla_norm_g, w_branch, w_gate, b_gate, w_out):
    bp, t_len, d = x_prompt.shape
    bs = x_sample.shape[0]
    depth = norm_g.shape[0]
    P = _prep(t_len, norm_g, w_ff_up, w_ff_down, w_in, hgrn_lb_logits, hgrn_norm_g,
              rglru_conv_w, rglru_conv_b, rglru_w_a, rglru_b_a, rglru_w_x, rglru_b_x, rglru_lambda,
              ret_norm_g, gla_w_up, gla_b_up, gla_norm_g, w_branch, w_gate, b_gate, w_out)
    st = (state_hgrn, state_rglru_h, jnp.swapaxes(state_rglru_conv, 1, 2), state_retention, state_gla)

    x = jnp.concatenate([x_prompt.reshape(bp * t_len, d), x_sample.reshape(bs, d)], axis=0)
    h = _rmsnorm_rows(x, norm_g[0, 0:1])
    new_p, new_s = [], []
    stacked = tuple(jnp.zeros_like(s) for s in (state_hgrn, state_retention, state_gla))
    for l in range(depth):
        x, h, stp, sts = _layer(x, h, st, stacked, l, P, bp, t_len, bs, l == depth - 1)
        stacked = (sts[0], sts[3], sts[4])
        new_p.append(stp)
        new_s.append(sts)
    yp = x[:bp * t_len].reshape(bp, t_len, d)
    ys = x[bp * t_len:].reshape(bs, 1, d)
    sp = [jnp.stack([s[i] for s in new_p]) for i in range(5)]
    s_lh = jnp.stack([s[1] for s in new_s])
    s_lc = jnp.stack([s[2] for s in new_s])
    return (yp, ys, sp[0], sp[1], sp[2], sp[3], sp[4], stacked[0], s_lh, s_lc, stacked[1], stacked[2])


def _prep(t_len, norm_g, w_ff_up, w_ff_down, w_in, hgrn_lb_logits, hgrn_norm_g,
          rglru_conv_w, rglru_conv_b, rglru_w_a, rglru_b_a, rglru_w_x, rglru_b_x, rglru_lambda,
          ret_norm_g, gla_w_up, gla_b_up, gla_norm_g, w_branch, w_gate, b_gate, w_out):
    row3 = lambda a: a.reshape(a.shape[0], 1, a.shape[1])
    cos_p, sin_p = _rope_tables(jnp.arange(t_len, dtype=jnp.int32))
    cos_s, sin_s = _rope_tables(PAST_LEN + jnp.arange(1, dtype=jnp.int32))
    P = dict(norm_g=norm_g, w_ff_up=w_ff_up, w_ff_down=w_ff_down, w_in_t=jnp.swapaxes(w_in, 1, 2),
             hgrn_lb_logits=hgrn_lb_logits, hgrn_g3=row3(hgrn_norm_g),
             rglru_conv_w=rglru_conv_w, conv_b3=row3(rglru_conv_b), rglru_w_a=rglru_w_a, b_a3=row3(rglru_b_a),
             rglru_w_x=rglru_w_x, b_x3=row3(rglru_b_x), lambda3=row3(rglru_lambda),
             ret_g3=row3(ret_norm_g), gla_g3=row3(gla_norm_g), b_up3=row3(gla_b_up),
             w_up_pad=jnp.pad(gla_w_up, ((0, 0), (0, LANES - GLA_RANK), (0, 0))),
             w_branch=w_branch, w_gate=w_gate, b_gate3=row3(b_gate), w_out=w_out,
             log_gamma=jnp.log1p(-jnp.exp2(-5.0 - jnp.arange(RET_HEADS, dtype=F32))),
             cos_p=cos_p, sin_p=sin_p, cos_s=cos_s, sin_s=sin_s)
    return P
```
